```python
import math
import jax, jax.numpy as jnp
from jax import lax
import numpy as np

D_MODEL = 1024
BATCH = 8
SEQ = 4096
DEPTH = 2

CHUNK = 64
EPS = 1e-6
SB_HEADS = 8
SB_HEAD_DIM = 64
SB_WIDTH = SB_HEADS * SB_HEAD_DIM
SB_BLOCK = 128
DN_HEADS = 4
DN_HEAD_DIM = 128
DN_WIDTH = DN_HEADS * DN_HEAD_DIM
CONV_WIDTH = 4
DN_CHUNK = CHUNK
ADA_SCALE = 0.3
IN_SIZES = [SB_WIDTH] * 4 + [DN_WIDTH] * 4 + [DN_HEADS] * 2 + [D_MODEL] * 2
IN_COLS = sum(IN_SIZES)
IN_SPLITS = [int(s) for s in np.cumsum(IN_SIZES)[:-1]]

kernel_name = "hybrid_stickbreak_gated_deltanet_block"


def rms_norm(x, g):
    xf = x.astype(jnp.float32)
    y = xf * lax.rsqrt(jnp.mean(xf * xf, axis=-1, keepdims=True) + EPS)
    return (y * g.astype(jnp.float32)).astype(x.dtype)


def l2_norm(x):
    xf = x.astype(jnp.float32)
    return xf * lax.rsqrt(jnp.sum(xf * xf, axis=-1, keepdims=True) + EPS)


def to_heads(t, n, d):
    b, s, _ = t.shape
    return t.reshape(b, s, n, d).transpose(0, 2, 1, 3)


def from_heads(t):
    b, h, s, d = t.shape
    return t.transpose(0, 2, 1, 3).reshape(b, s, h * d)


def stick_breaking_attention(q, k, v):
    S = q.shape[2]
    scale = 1.0 / math.sqrt(SB_HEAD_DIM)
    outs = []
    for start in range(0, S, SB_BLOCK):
        end = start + SB_BLOCK
        qb = q[:, :, start:end].astype(jnp.float32)
        kb = k[:, :, :end].astype(jnp.float32)
        vb = v[:, :, :end].astype(jnp.float32)
        z = jnp.einsum('bhqd,bhkd->bhqk', qb, kb) * scale
        q_pos = start + jnp.arange(SB_BLOCK)
        k_pos = jnp.arange(end)
        valid = k_pos[None, :] < q_pos[:, None]
        log_keep = jnp.where(valid, jax.nn.log_sigmoid(-z), 0.0)
        later = lax.cumsum(log_keep, axis=3, reverse=True) - log_keep
        w = jnp.where(valid, jnp.exp(jax.nn.log_sigmoid(z) + later), 0.0)
        outs.append(jnp.einsum('bhqk,bhkd->bhqd', w, vb))
    return jnp.concatenate(outs, axis=2).astype(v.dtype)


def causal_short_conv(x, w):
    K, C = w.shape
    out = lax.conv_general_dilated(x, w[:, None, :], window_strides=(1,), padding=[(K - 1, 0)],
                                   dimension_numbers=('NWC', 'WIO', 'NWC'), feature_group_count=C)
    return jax.nn.silu(out)


def gated_delta_rule(q, k, v, beta, g):
    B, H, S, Dk = q.shape
    Dv = v.shape[-1]
    C = DN_CHUNK
    N = S // C
    f32 = jnp.float32
    q = q.astype(f32) * (Dk ** -0.5)
    k, v, beta, g = k.astype(f32), v.astype(f32), beta.astype(f32), g.astype(f32)
    q, k, v = (t.reshape(B, H, N, C, t.shape[-1]) for t in (q, k, v))
    beta = beta.reshape(B, H, N, C)
    g = jnp.cumsum(g.reshape(B, H, N, C), axis=-1)
    tril = jnp.tril(jnp.ones((C, C), dtype=bool))
    strict = jnp.tril(jnp.ones((C, C), dtype=bool), -1)
    decay = jnp.exp(jnp.where(tril, g[..., :, None] - g[..., None, :], -jnp.inf))
    k_beta = k * beta[..., None]
    v_beta = v * beta[..., None]
    m = jnp.where(strict, jnp.einsum('bhnid,bhnjd->bhnij', k_beta, k) * decay, 0.0)
    u = lax.linalg.triangular_solve(m, v_beta, left_side=True, lower=True, unit_diagonal=True)
    w = lax.linalg.triangular_solve(m, k_beta * jnp.exp(g)[..., None], left_side=True,
                                    lower=True, unit_diagonal=True)
    intra = jnp.where(tril, jnp.einsum('bhnid,bhnjd->bhnij', q, k) * decay, 0.0)
    xs = tuple(jnp.moveaxis(t, 2, 0) for t in (q, k, u, w, g, intra))

    def step(state, inp):
        qc, kc, uc, wc, gc, ac = inp
        v_new = uc - jnp.einsum('bhcd,bhde->bhce', wc, state)
        o = (jnp.einsum('bhcd,bhde->bhce', qc * jnp.exp(gc)[..., None], state)
             + jnp.einsum('bhij,bhje->bhie', ac, v_new))
        g_last = gc[..., -1]
        state = (state * jnp.exp(g_last)[..., None, None]
                 + jnp.einsum('bhcd,bhce->bhde', kc * jnp.exp(g_last[..., None] - gc)[..., None], v_new))
        return state, o

    _, o = lax.scan(step, jnp.zeros((B, H, Dk, Dv), f32), xs)
    return jnp.moveaxis(o, 0, 2).reshape(B, H, S, Dv)


def hybrid_layer(x, c, ada_w, ada_b, norm_g, w_in, sb_q_g, sb_k_g, conv_w, dn_a_log,
                 dn_dt_bias, dn_norm_g, w_branch_sb, w_branch_dn, w_out):
    B, S, _ = x.shape
    mod = jax.nn.silu(c) @ ada_w + ada_b
    shift, scale, gate = jnp.split(mod, 3, axis=-1)
    h = rms_norm(x, norm_g) * (1.0 + scale[:, None]) + shift[:, None]

    p = h @ w_in
    (sb_q, sb_k, sb_v, sb_z, dn_q, dn_k, dn_v, dn_z, dn_b, dn_a,
     merge_sb, merge_dn) = jnp.split(p, IN_SPLITS, axis=-1)

    qa = rms_norm(to_heads(sb_q, SB_HEADS, SB_HEAD_DIM), sb_q_g)
    ka = rms_norm(to_heads(sb_k, SB_HEADS, SB_HEAD_DIM), sb_k_g)
    va = to_heads(sb_v, SB_HEADS, SB_HEAD_DIM)
    o_sb = from_heads(stick_breaking_attention(qa, ka, va)) * jax.nn.silu(sb_z)

    qkv = causal_short_conv(jnp.concatenate([dn_q, dn_k, dn_v], axis=-1), conv_w)
    qb, kb, vb = jnp.split(qkv, 3, axis=-1)
    qb = l2_norm(to_heads(qb, DN_HEADS, DN_HEAD_DIM))
    kb = l2_norm(to_heads(kb, DN_HEADS, DN_HEAD_DIM))
    vb = to_heads(vb, DN_HEADS, DN_HEAD_DIM)
    beta = jax.nn.sigmoid(dn_b.astype(jnp.float32)).transpose(0, 2, 1)
    log_decay = (-jnp.exp(dn_a_log.astype(jnp.float32))
                 * jax.nn.softplus(dn_a.astype(jnp.float32) + dn_dt_bias.astype(jnp.float32)))
    log_decay = log_decay.transpose(0, 2, 1)
    o_dn = gated_delta_rule(qb, kb, vb, beta, log_decay)
    o_dn = o_dn.transpose(0, 2, 1, 3)
    z_dn = dn_z.reshape(B, S, DN_HEADS, DN_HEAD_DIM).astype(jnp.float32)
    o_dn = (rms_norm(o_dn, dn_norm_g) * jax.nn.silu(z_dn)).reshape(B, S, DN_WIDTH).astype(x.dtype)

    y = (jax.nn.sigmoid(merge_sb) * (o_sb @ w_branch_sb)
         + jax.nn.sigmoid(merge_dn) * (o_dn @ w_branch_dn))
    out = y @ w_out
    return x + gate[:, None] * out


def setup_inputs(seed: int = 0) -> dict:
    key = jax.random.key(seed)
    ks = jax.random.split(key, 16)
    f32 = jnp.float32
    D = D_MODEL
    nrm = lambda k, shape, s: jax.random.normal(k, shape, f32) * s
    dt = jnp.exp(jax.random.uniform(ks[10], (DEPTH, DN_HEADS), f32, math.log(1e-3), math.log(1e-1)))
    return {
        "x": nrm(ks[0], (BATCH, SEQ, D), 1.0),
        "c": nrm(ks[1], (BATCH, D), 1.0),
        "ada_w": nrm(ks[2], (DEPTH, D, 3 * D), ADA_SCALE * D ** -0.5),
        "ada_b": nrm(ks[3], (DEPTH, 3 * D), 0.02),
        "norm_g": 1.0 + nrm(ks[4], (DEPTH, D), 0.02),
        "w_in": nrm(ks[5], (DEPTH, D, IN_COLS), D ** -0.5),
        "sb_q_g": 1.0 + nrm(ks[6], (DEPTH, SB_HEAD_DIM), 0.02),
        "sb_k_g": 1.0 + nrm(ks[7], (DEPTH, SB_HEAD_DIM), 0.02),
        "conv_w": nrm(ks[8], (DEPTH, CONV_WIDTH, 3 * DN_WIDTH), CONV_WIDTH ** -0.5),
        "dn_a_log": jnp.log(jax.random.uniform(ks[9], (DEPTH, DN_HEADS), f32, 1.0, 16.0)),
        "dn_dt_bias": dt + jnp.log(-jnp.expm1(-dt)),
        "dn_norm_g": 1.0 + nrm(ks[11], (DEPTH, DN_HEAD_DIM), 0.02),
        "w_branch_sb": nrm(ks[12], (DEPTH, SB_WIDTH, D), SB_WIDTH ** -0.5),
        "w_branch_dn": nrm(ks[13], (DEPTH, DN_WIDTH, D), DN_WIDTH ** -0.5),
        "w_out": nrm(ks[14], (DEPTH, D, D), D ** -0.5),
    }


def reference(x, c, ada_w, ada_b, norm_g, w_in, sb_q_g, sb_k_g, conv_w, dn_a_log,
              dn_dt_bias, dn_norm_g, w_branch_sb, w_branch_dn, w_out):
    for l in range(DEPTH):
        x = hybrid_layer(x, c, ada_w[l], ada_b[l], norm_g[l], w_in[l], sb_q_g[l], sb_k_g[l],
                         conv_w[l], dn_a_log[l], dn_dt_bias[l], dn_norm_g[l],
                         w_branch_sb[l], w_branch_dn[l], w_out[l])
    return x
```

```python
import functools

import jax
import jax.numpy as jnp
from jax import lax
from jax.experimental import pallas as pl
from jax.experimental.pallas import tpu as pltpu

F32 = jnp.float32
BF16 = jnp.bfloat16

EPS = 1e-6
SB_HEADS = 8
SB_HEAD_DIM = 64
SB_WIDTH = SB_HEADS * SB_HEAD_DIM
DN_HEADS = 4
DN_HEAD_DIM = 128
DN_WIDTH = DN_HEADS * DN_HEAD_DIM
CONV_WIDTH = 4

LANES = 128
HALO_ROWS = 8
GATE_ROWS = 16
TOKEN_TILE = 512
ATTN_BLOCK = 256
DN_BLOCK = 256
INV_BASE = 16
VMEM_LIMIT = 56 * 1024 * 1024
NEG_BIG = -1e30


def _dot(a, b):
    return jnp.dot(a, b, preferred_element_type=F32)


def _dot_nt(a, b):
    return lax.dot_general(a, b, (((1,), (1,)), ((), ())), preferred_element_type=F32)


def _sigmoid(x):
    return 1.0 / (1.0 + jnp.exp(-x))


def _softplus(x):
    return jnp.maximum(x, 0.0) + jnp.log1p(jnp.exp(-jnp.abs(x)))


def _block_of(idx, size):
    assert size & (size - 1) == 0
    return jnp.right_shift(idx, size.bit_length() - 1)


def _split3(x):
    hi = x.astype(BF16)
    r = x - hi.astype(F32)
    mid = r.astype(BF16)
    lo = (r - mid.astype(F32)).astype(BF16)
    return hi, mid, lo


def _adaln_kernel(c_ref, w_ref, b_ref, o_ref):
    c = c_ref[...]
    a = c * _sigmoid(c)
    w = w_ref[0]
    a_hi = a.astype(BF16)
    a_lo = (a - a_hi.astype(F32)).astype(BF16)
    w_hi = w.astype(BF16)
    w_lo = (w - w_hi.astype(F32)).astype(BF16)
    o_ref[0] = _dot(a_hi, w_hi) + _dot(a_hi, w_lo) + _dot(a_lo, w_hi) + b_ref[0]


def _adaln(c_pad, ada_w, ada_b):
    depth, d, d3 = ada_w.shape
    rows = c_pad.shape[0]
    tn = 512
    return pl.pallas_call(
        _adaln_kernel,
        out_shape=jax.ShapeDtypeStruct((depth, rows, d3), F32),
        grid=(depth, d3 // tn),
        in_specs=[
            pl.BlockSpec((rows, d), lambda l, j: (0, 0)),
            pl.BlockSpec((1, d, tn), lambda l, j: (l, 0, j)),
            pl.BlockSpec((1, 1, tn), lambda l, j: (l, 0, j)),
        ],
        out_specs=pl.BlockSpec((1, rows, tn), lambda l, j: (l, 0, j)),
        compiler_params=pltpu.CompilerParams(
            dimension_semantics=("arbitrary", "arbitrary"), vmem_limit_bytes=VMEM_LIMIT),
        name="adaln",
    )(c_pad, ada_w, ada_b.reshape(depth, 1, d3))


def _inproj_kernel(x_ref, xh_ref, sc_ref, sh_ref, ng_ref, wm_ref, wba_ref, wmg_ref, gq_ref, gk_ref,
                   grp_ref, cw_ref, alog_ref, dtb_ref,
                   q_ref, k_ref, vt_ref, z_ref, dq_ref, dk_ref, dkt_ref, dv_ref, dz_ref,
                   gc_ref, gr_ref, msb_ref, mdn_ref, *, tm, s_tiles):
    t = pl.program_id(0)
    gmod = ng_ref[...] * (1.0 + sc_ref[0])
    shift = sh_ref[0]

    def norm_mod(xv):
        ms = jnp.mean(xv * xv, axis=-1, keepdims=True)
        return (xv * lax.rsqrt(ms + EPS) * gmod + shift).astype(BF16)

    hb = norm_mod(x_ref[...])
    hh = norm_mod(xh_ref[...])
    seq_start = (t % s_tiles) == 0

    def group(g):
        return _dot(hb, wm_ref[:, g * SB_WIDTH:(g + 1) * SB_WIDTH])

    def head_rms(p, gain):
        ms = _dot((p * p).astype(BF16), grp_ref[...])
        return p * lax.rsqrt(ms + EPS) * gain

    n_sub = tm // ATTN_BLOCK

    def store_transposed(ref, val):
        for a in range(n_sub):
            ref[0, a] = val[a * ATTN_BLOCK:(a + 1) * ATTN_BLOCK, :].T.astype(BF16)

    q_ref[...] = head_rms(group(0), gq_ref[...]).astype(BF16)
    k_ref[...] = head_rms(group(1), gk_ref[...]).astype(BF16)
    store_transposed(vt_ref, group(2))
    pz = group(3)
    z_ref[...] = (pz * _sigmoid(pz)).astype(BF16)

    row = lax.broadcasted_iota(jnp.int32, (tm, 1), 0)

    def conv_silu(g, cw):
        cols = slice(g * SB_WIDTH, (g + 1) * SB_WIDTH)
        p = _dot(hb, wm_ref[:, cols])
        ph = jnp.where(seq_start, 0.0, _dot(hh, wm_ref[:, cols]))
        acc = p * cw[CONV_WIDTH - 1:CONV_WIDTH, :]
        for back in range(1, CONV_WIDTH):
            shifted = pltpu.roll(p, back, axis=0)
            halo = pltpu.roll(ph, back, axis=0)
            halo_full = jnp.concatenate([halo, jnp.zeros((tm - HALO_ROWS, SB_WIDTH), F32)], axis=0)
            src = jnp.where(row < back, halo_full, shifted)
            acc = acc + src * cw[CONV_WIDTH - 1 - back:CONV_WIDTH - back, :]
        return acc * _sigmoid(acc)

    def l2n(y, scale):
        outs = []
        for h in range(DN_HEADS):
            yh = y[:, h * DN_HEAD_DIM:(h + 1) * DN_HEAD_DIM]
            ss = jnp.sum(yh * yh, axis=-1, keepdims=True)
            outs.append(yh * (lax.rsqrt(ss + EPS) * scale))
        return jnp.concatenate(outs, axis=1)

    cw = cw_ref[...]
    yq = conv_silu(4, cw[:, 0:DN_WIDTH])
    dq_ref[...] = l2n(yq, DN_HEAD_DIM ** -0.5).astype(BF16)
    yk = l2n(conv_silu(5, cw[:, DN_WIDTH:2 * DN_WIDTH]), 1.0)
    dk_ref[...] = yk.astype(BF16)
    store_transposed(dkt_ref, yk)
    dv_ref[...] = conv_silu(6, cw[:, 2 * DN_WIDTH:3 * DN_WIDTH]).astype(BF16)
    pz = group(7)
    dz_ref[...] = (pz * _sigmoid(pz)).astype(BF16)

    pba = _dot(hb, wba_ref[...])
    beta = _sigmoid(pba)
    ld = -jnp.exp(alog_ref[...]) * _softplus(pba + dtb_ref[...])
    lane = lax.broadcasted_iota(jnp.int32, (tm, LANES), 1)
    ld = jnp.where((lane >= DN_HEADS) & (lane < 3 * DN_HEADS), ld, 0.0)
    ri = lax.broadcasted_iota(jnp.int32, (tm, tm), 0)
    ci = lax.broadcasted_iota(jnp.int32, (tm, tm), 1)
    same = _block_of(ri, DN_BLOCK) == _block_of(ci, DN_BLOCK)
    lower = jnp.where(same & (ci <= ri), 1.0, 0.0).astype(BF16)
    whole = jnp.where(same, 1.0, 0.0).astype(BF16)
    parts = _split3(ld)
    gcum = _dot(lower, parts[0]) + _dot(lower, parts[1]) + _dot(lower, parts[2])
    gtot = _dot(whole, parts[0]) + _dot(whole, parts[1]) + _dot(whole, parts[2])
    gates = jnp.where(lane < DN_HEADS, beta, jnp.where(lane < 2 * DN_HEADS, gcum, gtot))
    gc_ref[...] = gates
    gr_ref[0] = gates.T[0:GATE_ROWS, :]

    d_model = msb_ref.shape[1]
    msb_ref[...] = _sigmoid(_dot(hb, wmg_ref[:, 0:d_model])).astype(BF16)
    mdn_ref[...] = _sigmoid(_dot(hb, wmg_ref[:, d_model:2 * d_model])).astype(BF16)


def _inproj(x2, scale, shift, norm_g, wm, wba, wmg, gq, gk, grp, cw, alog, dtb, *, batch, seq):
    n, d = x2.shape
    tm = TOKEN_TILE
    s_tiles = seq // tm
    n_sub = tm // ATTN_BLOCK
    n_blk = seq // ATTN_BLOCK
    tile = lambda w: pl.BlockSpec((tm, w), lambda t: (t, 0))
    full = lambda a: pl.BlockSpec(a.shape, lambda t: (0,) * a.ndim)
    per_batch = pl.BlockSpec((1, 1, d), lambda t: (t // s_tiles, 0, 0))
    tr_spec = pl.BlockSpec((1, n_sub, SB_WIDTH, ATTN_BLOCK), lambda t: (t // s_tiles, t % s_tiles, 0, 0))
    tr_shape = jax.ShapeDtypeStruct((batch, n_blk, SB_WIDTH, ATTN_BLOCK), BF16)
    tok = lambda w, dt=BF16: jax.ShapeDtypeStruct((n, w), dt)
    out_shape = (
        tok(SB_WIDTH), tok(SB_WIDTH), tr_shape, tok(SB_WIDTH),
        tok(DN_WIDTH), tok(DN_WIDTH), tr_shape, tok(DN_WIDTH), tok(DN_WIDTH),
        tok(LANES, F32),
        jax.ShapeDtypeStruct((batch, GATE_ROWS, seq), F32),
        tok(d), tok(d),
    )
    out_specs = (
        tile(SB_WIDTH), tile(SB_WIDTH), tr_spec, tile(SB_WIDTH),
        tile(DN_WIDTH), tile(DN_WIDTH), tr_spec, tile(DN_WIDTH), tile(DN_WIDTH),
        tile(LANES),
        pl.BlockSpec((1, GATE_ROWS, tm), lambda t: (t // s_tiles, 0, t % s_tiles)),
        tile(d), tile(d),
    )
    halo_blocks = tm // HALO_ROWS
    in_specs = [
        tile(d),
        pl.BlockSpec((HALO_ROWS, d), lambda t: (jnp.maximum(t * halo_blocks - 1, 0), 0)),
        per_batch, per_batch, full(norm_g), full(wm), full(wba), full(wmg), full(gq), full(gk),
        full(grp), full(cw), full(alog), full(dtb),
    ]
    return pl.pallas_call(
        functools.partial(_inproj_kernel, tm=tm, s_tiles=s_tiles),
        out_shape=out_shape,
        grid=(n // tm,),
        in_specs=in_specs,
        out_specs=out_specs,
        compiler_params=pltpu.CompilerParams(
            dimension_semantics=("arbitrary",), vmem_limit_bytes=VMEM_LIMIT),
        name="inproj",
    )(x2, x2, scale, shift, norm_g, wm, wba, wmg, gq, gk, grp, cw, alog, dtb)


def _attn_kernel(q_ref, k_ref, vt_ref, z_ref, o_ref, *, blk):
    i = pl.program_id(2)
    ks = lax.broadcasted_iota(jnp.int32, (blk, blk), 0)
    qt = lax.broadcasted_iota(jnp.int32, (blk, blk), 1)
    valid = ks < qt
    upper = jnp.where(qt > ks, 1.0, 0.0).astype(BF16)
    lane = lax.broadcasted_iota(jnp.int32, (blk, LANES), 1)
    q = q_ref[...]
    heads_per_step = LANES // SB_HEAD_DIM

    def block(j, hh, qh, carry, acc, masked):
        kj = k_ref[0, pl.ds(pl.multiple_of(j * blk, blk), blk), :]
        st = _dot_nt(kj, qh)
        sp_all = _softplus(st)
        sp = jnp.where(valid, sp_all, 0.0) if masked else sp_all
        hi = sp.astype(BF16)
        lo = (sp - hi.astype(F32)).astype(BF16)
        later = _dot(upper, hi) + _dot(upper, lo)
        w = jnp.exp(st - sp_all - later - carry)
        if masked:
            w = jnp.where(valid, w, 0.0)
        vj = vt_ref[0, j, hh * SB_HEAD_DIM:(hh + 1) * SB_HEAD_DIM, :]
        acc = acc + _dot(vj, w.astype(BF16))
        carry = carry + later[0:1, :] + sp[0:1, :]
        return carry, acc

    accs = []
    for hh in range(heads_per_step):
        qh = jnp.where(_block_of(lane, SB_HEAD_DIM) == hh, q, jnp.zeros_like(q))
        carry0 = jnp.zeros((1, blk), F32)
        acc0 = jnp.zeros((SB_HEAD_DIM, blk), F32)
        carry, acc = block(i, hh, qh, carry0, acc0, True)

        def body(it, state, hh=hh, qh=qh):
            return block(i - 1 - it, hh, qh, state[0], state[1], False)

        carry, acc = lax.fori_loop(0, i, body, (carry, acc))
        accs.append(acc)
    o = jnp.concatenate(accs, axis=0).T
    o_ref[...] = (o * z_ref[...].astype(F32)).astype(BF16)


def _attention(q, k, vt, z, *, batch, seq):
    blk = ATTN_BLOCK
    n_blk = seq // blk
    pairs = SB_WIDTH // LANES
    k3 = k.reshape(batch, seq, SB_WIDTH)
    qspec = pl.BlockSpec((blk, LANES), lambda b, p, i: (b * n_blk + i, p))
    return pl.pallas_call(
        functools.partial(_attn_kernel, blk=blk),
        out_shape=jax.ShapeDtypeStruct(q.shape, BF16),
        grid=(batch, pairs, n_blk),
        in_specs=[
            qspec,
            pl.BlockSpec((1, seq, LANES), lambda b, p, i: (b, 0, p)),
            pl.BlockSpec((1, n_blk, LANES, blk), lambda b, p, i: (b, 0, p, 0)),
            qspec,
        ],
        out_specs=qspec,
        compiler_params=pltpu.CompilerParams(
            dimension_semantics=("arbitrary", "arbitrary", "arbitrary"), vmem_limit_bytes=VMEM_LIMIT),
        name="sb_attention",
    )(q, k3, vt, z)


def _inverse_minus_identity(m, ri, ci, size):
    mm = lambda a, b: _dot(a.astype(BF16), b.astype(BF16))
    base = _block_of(ri, INV_BASE) == _block_of(ci, INV_BASE)
    md = jnp.where(base, m, 0.0)
    m2 = mm(md, md)
    m4 = mm(m2, m2)
    m8 = mm(m4, m4)
    n = m2 - md - mm(md, m2)
    n = n + m4 + mm(n, m4)
    n = n + m8 + mm(n, m8)
    width = INV_BASE
    while width < size:
        inner = _block_of(ri, width) == _block_of(ci, width)
        outer = _block_of(ri, 2 * width) == _block_of(ci, 2 * width)
        mo = jnp.where(outer & jnp.logical_not(inner), m, 0.0)
        x = mo + mm(n, mo)
        n = n - x - mm(x, n)
        width *= 2
    return n


def _dn_kernel(q_ref, k_ref, kt_ref, v_ref, z_ref, gc_ref, gr_ref, ng_ref, o_ref, state_ref, *, blk):
    @pl.when(pl.program_id(1) == 0)
    def _():
        state_ref[...] = jnp.zeros_like(state_ref)

    ri = lax.broadcasted_iota(jnp.int32, (blk, blk), 0)
    ci = lax.broadcasted_iota(jnp.int32, (blk, blk), 1)
    tril = ci <= ri
    strict = ci < ri
    gates = gc_ref[...]
    grows = gr_ref[0]
    for h in range(DN_HEADS):
        cols = slice(h * DN_HEAD_DIM, (h + 1) * DN_HEAD_DIM)
        q = q_ref[:, cols]
        k = k_ref[:, cols]
        kf = k.astype(F32)
        beta = gates[:, h:h + 1]
        g = gates[:, DN_HEADS + h:DN_HEADS + h + 1]
        g_row = grows[DN_HEADS + h:DN_HEADS + h + 1, :]
        gt_row = grows[2 * DN_HEADS + h:2 * DN_HEADS + h + 1, :]
        eg = jnp.exp(g)
        kb = kf * beta
        vb = v_ref[:, cols].astype(F32) * beta
        kbg = kb * eg
        decay = jnp.exp(jnp.where(tril, g - g_row, NEG_BIG))
        m = jnp.where(strict, _dot_nt(kb.astype(BF16), k) * decay, 0.0)
        n = _inverse_minus_identity(m, ri, ci, blk).astype(BF16)
        u = vb + _dot(n, vb.astype(BF16))
        w = kbg + _dot(n, kbg.astype(BF16))
        intra = _dot_nt(q, k) * decay

        state = state_ref[h]
        sb = state.astype(BF16)
        v_new = u - _dot(w.astype(BF16), sb)
        qe = q.astype(F32) * eg
        o = _dot(qe.astype(BF16), sb) + _dot(intra.astype(BF16), v_new.astype(BF16))
        ktd = (kt_ref[0, 0, cols, :].astype(F32) * jnp.exp(gt_row - g_row)).astype(BF16)
        state_ref[h] = state * jnp.exp(gt_row[:, 0:1]) + _dot(ktd, v_new.astype(BF16))

        ms = jnp.mean(o * o, axis=-1, keepdims=True)
        on = o * lax.rsqrt(ms + EPS) * ng_ref[...]
        o_ref[:, cols] = (on * z_ref[:, cols].astype(F32)).astype(BF16)


def _deltanet(dq, dk, dkt, dv, dz, gc, gr, ng, *, batch, seq):
    blk = DN_BLOCK
    n_blk = seq // blk
    tok = pl.BlockSpec((blk, DN_WIDTH), lambda b, c: (b * n_blk + c, 0))
    return pl.pallas_call(
        functools.partial(_dn_kernel, blk=blk),
        out_shape=jax.ShapeDtypeStruct(dq.shape, BF16),
        grid=(batch, n_blk),
        in_specs=[
            tok, tok,
            pl.BlockSpec((1, 1, DN_WIDTH, blk), lambda b, c: (b, c, 0, 0)),
            tok, tok,
            pl.BlockSpec((blk, LANES), lambda b, c: (b * n_blk + c, 0)),
            pl.BlockSpec((1, GATE_ROWS, blk), lambda b, c: (b, 0, c)),
            pl.BlockSpec((1, DN_HEAD_DIM), lambda b, c: (0, 0)),
        ],
        out_specs=tok,
        scratch_shapes=[pltpu.VMEM((DN_HEADS, DN_HEAD_DIM, DN_HEAD_DIM), F32)],
        compiler_params=pltpu.CompilerParams(
            dimension_semantics=("arbitrary", "arbitrary"), vmem_limit_bytes=VMEM_LIMIT),
        name="deltanet",
    )(dq, dk, dkt, dv, dz, gc, gr, ng)


def _outproj_kernel(x_ref, osb_ref, odn_ref, msb_ref, mdn_ref, gate_ref, wsb_ref, wdn_ref, wo_ref, o_ref):
    ysb = _dot(osb_ref[...], wsb_ref[...])
    ydn = _dot(odn_ref[...], wdn_ref[...])
    y = msb_ref[...].astype(F32) * ysb + mdn_ref[...].astype(F32) * ydn
    out = _dot(y.astype(BF16), wo_ref[...])
    o_ref[...] = x_ref[...] + gate_ref[0] * out


def _outproj(x2, osb, odn, msb, mdn, gate, wsb, wdn, wo, *, seq):
    n, d = x2.shape
    tm = TOKEN_TILE
    s_tiles = seq // tm
    tile = lambda w: pl.BlockSpec((tm, w), lambda t: (t, 0))
    full = lambda a: pl.BlockSpec(a.shape, lambda t: (0,) * a.ndim)
    return pl.pallas_call(
        _outproj_kernel,
        out_shape=jax.ShapeDtypeStruct((n, d), F32),
        grid=(n // tm,),
        in_specs=[tile(d), tile(SB_WIDTH), tile(DN_WIDTH), tile(d), tile(d),
                  pl.BlockSpec((1, 1, d), lambda t: (t // s_tiles, 0, 0)),
                  full(wsb), full(wdn), full(wo)],
        out_specs=tile(d),
        compiler_params=pltpu.CompilerParams(
            dimension_semantics=("arbitrary",), vmem_limit_bytes=VMEM_LIMIT),
        name="outproj",
    )(x2, osb, odn, msb, mdn, gate, wsb, wdn, wo)


def _layer(x2, mod, norm_g, w_in, sb_q_g, sb_k_g, conv_w, dn_a_log, dn_dt_bias, dn_norm_g,
           w_branch_sb, w_branch_dn, w_out, *, batch, seq):
    d = x2.shape[1]
    shift = mod[:batch, 0:d].reshape(batch, 1, d)
    scale = mod[:batch, d:2 * d].reshape(batch, 1, d)
    gate = mod[:batch, 2 * d:3 * d].reshape(batch, 1, d)

    main_cols = 4 * SB_WIDTH + 4 * DN_WIDTH
    wm = w_in[:, :main_cols].astype(BF16)
    w_b = w_in[:, main_cols:main_cols + DN_HEADS]
    w_a = w_in[:, main_cols + DN_HEADS:main_cols + 2 * DN_HEADS]
    pad = jnp.zeros((d, LANES - 3 * DN_HEADS), F32)
    wba = jnp.concatenate([w_b, w_a, w_a, pad], axis=1).astype(BF16)
    wmg = w_in[:, main_cols + 2 * DN_HEADS:].astype(BF16)

    def gate_lanes(v):
        z = jnp.zeros((DN_HEADS,), F32)
        return jnp.concatenate([z, v, v, jnp.zeros((LANES - 3 * DN_HEADS,), F32)]).reshape(1, LANES)

    gq = (jnp.tile(sb_q_g, SB_HEADS) * (SB_HEAD_DIM ** -0.5)).reshape(1, SB_WIDTH)
    gk = jnp.tile(sb_k_g, SB_HEADS).reshape(1, SB_WIDTH)
    head_of = jnp.arange(SB_WIDTH) // SB_HEAD_DIM
    grp = jnp.where(head_of[:, None] == head_of[None, :], 1.0 / SB_HEAD_DIM, 0.0).astype(BF16)

    (q, k, vt, z, dq, dk, dkt, dv, dz, gc, gr, msb, mdn) = _inproj(
        x2, scale, shift, norm_g.reshape(1, d), wm, wba, wmg, gq, gk, grp, conv_w,
        gate_lanes(dn_a_log), gate_lanes(dn_dt_bias), batch=batch, seq=seq)
    osb = _attention(q, k, vt, z, batch=batch, seq=seq)
    odn = _deltanet(dq, dk, dkt, dv, dz, gc, gr, dn_norm_g.reshape(1, DN_HEAD_DIM), batch=batch, seq=seq)
    return _outproj(x2, osb, odn, msb, mdn, gate, w_branch_sb.astype(BF16), w_branch_dn.astype(BF16),
                    w_out.astype(BF16), seq=seq)


def kernel(x, c, ada_w, ada_b, norm_g, w_in, sb_q_g, sb_k_g, conv_w, dn_a_log, dn_dt_bias, dn_norm_g,
           w_branch_sb, w_branch_dn, w_out):
    batch, seq, d = x.shape
    depth = ada_w.shape[0]
    assert seq % TOKEN_TILE == 0 and TOKEN_TILE % ATTN_BLOCK == 0 and ATTN_BLOCK == DN_BLOCK
    rows = 16
    c_pad = jnp.concatenate([c, jnp.zeros((rows - batch, d), c.dtype)], axis=0) if batch < rows else c
    mod = _adaln(c_pad, ada_w, ada_b)
    x2 = x.reshape(batch * seq, d)
    for l in range(depth):
        x2 = _layer(x2, mod[l], norm_g[l], w_in[l], sb_q_g[l], sb_k_g[l], conv_w[l], dn_a_log[l],
                    dn_dt_bias[l], dn_norm_g[l], w_branch_sb[l], w_branch_dn[l], w_out[l],
                    batch=batch, seq=seq)
    return x2.reshape(batch, seq, d)
```

```python
import functools

import jax
import jax.numpy as jnp
from jax import lax
from jax.experimental import pallas as pl
from jax.experimental.pallas import tpu as pltpu

F32 = jnp.float32
BF16 = jnp.bfloat16

EPS = 1e-6
SB_HEADS = 8
SB_HEAD_DIM = 64
SB_WIDTH = SB_HEADS * SB_HEAD_DIM
DN_HEADS = 4
DN_HEAD_DIM = 128
DN_WIDTH = DN_HEADS * DN_HEAD_DIM
CONV_WIDTH = 4

LANES = 128
HALO_ROWS = 8
GATE_ROWS = 16
TOKEN_TILE = 512
ATTN_Q = 256
TR_BLOCK = 128
DN_BLOCK = 256
INV_BASE = 16
VMEM_LIMIT = 56 * 1024 * 1024
NEG_BIG = -1e30
LOG2E = 1.4426950408889634
EXIT_LOG2 = 152.0


def _dot(a, b):
    return jnp.dot(a, b, preferred_element_type=F32)


def _dot_nt(a, b):
    return lax.dot_general(a, b, (((1,), (1,)), ((), ())), preferred_element_type=F32)


def _sigmoid(x):
    return 1.0 / (1.0 + jnp.exp(-x))


def _neg_abs(x):
    bits = lax.bitcast_convert_type(x, jnp.uint32) | jnp.uint32(0x80000000)
    return lax.bitcast_convert_type(bits, F32)


def _softplus(x):
    return jnp.maximum(x, 0.0) + jnp.log1p(jnp.exp(-jnp.abs(x)))


def _block_of(idx, size):
    assert size & (size - 1) == 0
    return jnp.right_shift(idx, size.bit_length() - 1)


def _split3(x):
    hi = x.astype(BF16)
    r = x - hi.astype(F32)
    mid = r.astype(BF16)
    lo = (r - mid.astype(F32)).astype(BF16)
    return hi, mid, lo


def _adaln_kernel(c_ref, w_ref, b_ref, o_ref):
    c = c_ref[...]
    a = c * _sigmoid(c)
    w = w_ref[0]
    a_hi = a.astype(BF16)
    a_lo = (a - a_hi.astype(F32)).astype(BF16)
    w_hi = w.astype(BF16)
    w_lo = (w - w_hi.astype(F32)).astype(BF16)
    o_ref[0] = _dot(a_hi, w_hi) + _dot(a_hi, w_lo) + _dot(a_lo, w_hi) + b_ref[0]


def _adaln(c_pad, ada_w, ada_b):
    depth, d, d3 = ada_w.shape
    rows = c_pad.shape[0]
    tn = 512
    return pl.pallas_call(
        _adaln_kernel,
        out_shape=jax.ShapeDtypeStruct((depth, rows, d3), F32),
        grid=(depth, d3 // tn),
        in_specs=[
            pl.BlockSpec((rows, d), lambda l, j: (0, 0)),
            pl.BlockSpec((1, d, tn), lambda l, j: (l, 0, j)),
            pl.BlockSpec((1, 1, tn), lambda l, j: (l, 0, j)),
        ],
        out_specs=pl.BlockSpec((1, rows, tn), lambda l, j: (l, 0, j)),
        compiler_params=pltpu.CompilerParams(
            dimension_semantics=("arbitrary", "arbitrary"), vmem_limit_bytes=VMEM_LIMIT),
        name="adaln",
    )(c_pad, ada_w, ada_b.reshape(depth, 1, d3))


def _inproj_kernel(x_ref, xh_ref, sc_ref, sh_ref, ng_ref, wm_ref, wba_ref, wmg_ref, gq_ref, gk_ref,
                   grp_ref, cw_ref, alog_ref, dtb_ref,
                   q_ref, k_ref, vt_ref, z_ref, dq_ref, dk_ref, dkt_ref, dv_ref, dz_ref,
                   gc_ref, gr_ref, msb_ref, mdn_ref, *, tm, s_tiles):
    t = pl.program_id(0)
    gmod = ng_ref[...] * (1.0 + sc_ref[0])
    shift = sh_ref[0]

    def norm_mod(xv):
        ms = jnp.mean(xv * xv, axis=-1, keepdims=True)
        return (xv * lax.rsqrt(ms + EPS) * gmod + shift).astype(BF16)

    hb = norm_mod(x_ref[...])
    hh = norm_mod(xh_ref[...])
    seq_start = (t % s_tiles) == 0

    def group(g):
        return _dot(hb, wm_ref[:, g * SB_WIDTH:(g + 1) * SB_WIDTH])

    def head_rms(p, gain):
        ms = _dot((p * p).astype(BF16), grp_ref[...])
        return p * lax.rsqrt(ms + EPS) * gain

    def store_transposed(ref, val):
        for a in range(tm // TR_BLOCK):
            ref[0, a] = val[a * TR_BLOCK:(a + 1) * TR_BLOCK, :].T.astype(BF16)

    q_ref[...] = head_rms(group(0), gq_ref[...]).astype(BF16)
    k_ref[...] = head_rms(group(1), gk_ref[...]).astype(BF16)
    store_transposed(vt_ref, group(2))
    pz = group(3)
    z_ref[...] = (pz * _sigmoid(pz)).astype(BF16)

    row = lax.broadcasted_iota(jnp.int32, (tm, 1), 0)

    def conv_silu(g, cw):
        cols = slice(g * SB_WIDTH, (g + 1) * SB_WIDTH)
        p = _dot(hb, wm_ref[:, cols])
        ph = jnp.where(seq_start, 0.0, _dot(hh, wm_ref[:, cols]))
        acc = p * cw[CONV_WIDTH - 1:CONV_WIDTH, :]
        for back in range(1, CONV_WIDTH):
            shifted = pltpu.roll(p, back, axis=0)
            halo = pltpu.roll(ph, back, axis=0)
            halo_full = jnp.concatenate([halo, jnp.zeros((tm - HALO_ROWS, SB_WIDTH), F32)], axis=0)
            src = jnp.where(row < back, halo_full, shifted)
            acc = acc + src * cw[CONV_WIDTH - 1 - back:CONV_WIDTH - back, :]
        return acc * _sigmoid(acc)

    def l2n(y, scale):
        outs = []
        for h in range(DN_HEADS):
            yh = y[:, h * DN_HEAD_DIM:(h + 1) * DN_HEAD_DIM]
            ss = jnp.sum(yh * yh, axis=-1, keepdims=True)
            outs.append(yh * (lax.rsqrt(ss + EPS) * scale))
        return jnp.concatenate(outs, axis=1)

    cw = cw_ref[...]
    yq = conv_silu(4, cw[:, 0:DN_WIDTH])
    dq_ref[...] = l2n(yq, DN_HEAD_DIM ** -0.5).astype(BF16)
    yk = l2n(conv_silu(5, cw[:, DN_WIDTH:2 * DN_WIDTH]), 1.0)
    dk_ref[...] = yk.astype(BF16)
    store_transposed(dkt_ref, yk)
    dv_ref[...] = conv_silu(6, cw[:, 2 * DN_WIDTH:3 * DN_WIDTH]).astype(BF16)
    pz = group(7)
    dz_ref[...] = (pz * _sigmoid(pz)).astype(BF16)

    pba = _dot(hb, wba_ref[...])
    beta = _sigmoid(pba)
    ld = -jnp.exp(alog_ref[...]) * _softplus(pba + dtb_ref[...])
    lane = lax.broadcasted_iota(jnp.int32, (tm, LANES), 1)
    ld = jnp.where((lane >= DN_HEADS) & (lane < 3 * DN_HEADS), ld, 0.0)
    ri = lax.broadcasted_iota(jnp.int32, (tm, tm), 0)
    ci = lax.broadcasted_iota(jnp.int32, (tm, tm), 1)
    same = _block_of(ri, DN_BLOCK) == _block_of(ci, DN_BLOCK)
    lower = jnp.where(same & (ci <= ri), 1.0, 0.0).astype(BF16)
    whole = jnp.where(same, 1.0, 0.0).astype(BF16)
    parts = _split3(ld)
    gcum = _dot(lower, parts[0]) + _dot(lower, parts[1]) + _dot(lower, parts[2])
    gtot = _dot(whole, parts[0]) + _dot(whole, parts[1]) + _dot(whole, parts[2])
    gates = jnp.where(lane < DN_HEADS, beta, jnp.where(lane < 2 * DN_HEADS, gcum, gtot))
    gc_ref[...] = gates
    gr_ref[0] = gates.T[0:GATE_ROWS, :]

    d_model = msb_ref.shape[1]
    msb_ref[...] = _sigmoid(_dot(hb, wmg_ref[:, 0:d_model])).astype(BF16)
    mdn_ref[...] = _sigmoid(_dot(hb, wmg_ref[:, d_model:2 * d_model])).astype(BF16)


def _inproj(x2, scale, shift, norm_g, wm, wba, wmg, gq, gk, grp, cw, alog, dtb, *, batch, seq):
    n, d = x2.shape
    tm = TOKEN_TILE
    s_tiles = seq // tm
    n_sub = tm // TR_BLOCK
    n_blk = seq // TR_BLOCK
    tile = lambda w: pl.BlockSpec((tm, w), lambda t: (t, 0))
    full = lambda a: pl.BlockSpec(a.shape, lambda t: (0,) * a.ndim)
    per_batch = pl.BlockSpec((1, 1, d), lambda t: (t // s_tiles, 0, 0))
    tr_spec = pl.BlockSpec((1, n_sub, SB_WIDTH, TR_BLOCK), lambda t: (t // s_tiles, t % s_tiles, 0, 0))
    tr_shape = jax.ShapeDtypeStruct((batch, n_blk, SB_WIDTH, TR_BLOCK), BF16)
    tok = lambda w, dt=BF16: jax.ShapeDtypeStruct((n, w), dt)
    out_shape = (
        tok(SB_WIDTH), tok(SB_WIDTH), tr_shape, tok(SB_WIDTH),
        tok(DN_WIDTH), tok(DN_WIDTH), tr_shape, tok(DN_WIDTH), tok(DN_WIDTH),
        tok(LANES, F32),
        jax.ShapeDtypeStruct((batch, GATE_ROWS, seq), F32),
        tok(d), tok(d),
    )
    out_specs = (
        tile(SB_WIDTH), tile(SB_WIDTH), tr_spec, tile(SB_WIDTH),
        tile(DN_WIDTH), tile(DN_WIDTH), tr_spec, tile(DN_WIDTH), tile(DN_WIDTH),
        tile(LANES),
        pl.BlockSpec((1, GATE_ROWS, tm), lambda t: (t // s_tiles, 0, t % s_tiles)),
        tile(d), tile(d),
    )
    halo_blocks = tm // HALO_ROWS
    in_specs = [
        tile(d),
        pl.BlockSpec((HALO_ROWS, d), lambda t: (jnp.maximum(t * halo_blocks - 1, 0), 0)),
        per_batch, per_batch, full(norm_g), full(wm), full(wba), full(wmg), full(gq), full(gk),
        full(grp), full(cw), full(alog), full(dtb),
    ]
    return pl.pallas_call(
        functools.partial(_inproj_kernel, tm=tm, s_tiles=s_tiles),
        out_shape=out_shape,
        grid=(n // tm,),
        in_specs=in_specs,
        out_specs=out_specs,
        compiler_params=pltpu.CompilerParams(
            dimension_semantics=("arbitrary",), vmem_limit_bytes=VMEM_LIMIT),
        name="inproj",
    )(x2, x2, scale, shift, norm_g, wm, wba, wmg, gq, gk, grp, cw, alog, dtb)


def _attn_kernel(q_ref, k_ref, vt_ref, z_ref, o_ref, *, tq, tk):
    i = pl.program_id(2)
    heads = LANES // SB_HEAD_DIM
    ks = lax.broadcasted_iota(jnp.int32, (tk, tq), 0)
    qt = lax.broadcasted_iota(jnp.int32, (tk, tq), 1)
    causal = ks < qt
    ur = lax.broadcasted_iota(jnp.int32, (tk, tk), 0)
    uc = lax.broadcasted_iota(jnp.int32, (tk, tk), 1)
    upper = jnp.where(uc > ur, 1.0, 0.0).astype(BF16)
    lane = lax.broadcasted_iota(jnp.int32, (tq, LANES), 1)
    q = q_ref[...]
    qhs = [jnp.where(_block_of(lane, SB_HEAD_DIM) == hh, q, jnp.zeros_like(q)) for hh in range(heads)]

    def scores(units):
        st = [_dot_nt(k_ref[0, pl.ds(pl.multiple_of(j * tk, tk), tk), :], qh) for j, qh, _ in units]
        sp = [jnp.maximum(s, 0.0) + jnp.log(1.0 + jnp.exp2(_neg_abs(s))) * LOG2E for s in st]
        spm = [p if u[2] is None else jnp.where(u[2], p, 0.0) for p, u in zip(sp, units)]
        later = [_dot(upper, m.astype(BF16)) for m in spm]
        base = [s - p - l for s, p, l in zip(st, sp, later)]
        sums = [l[0:1, :] + m[0:1, :] for l, m in zip(later, spm)]
        return base, sums

    def weights(base, carry, mask):
        w = jnp.exp2(base - carry)
        if mask is not None:
            w = jnp.where(mask, w, 0.0)
        return w.astype(BF16)

    def values(j, hh):
        return vt_ref[0, j, hh * SB_HEAD_DIM:(hh + 1) * SB_HEAD_DIM, :]

    j_hi = 2 * i + 1
    j_lo = 2 * i
    causal_hi = causal[:, :tk]

    def diagonal_units():
        units = []
        for hh in range(heads):
            units += [(j_hi, qhs[hh][tk:, :], causal_hi), (j_lo, qhs[hh], causal)]
        return units

    def diagonal_weights(base, sums):
        w_hi, w_lo, carry = [], [], []
        for hh in range(heads):
            carry_hi = jnp.concatenate([jnp.zeros((1, tk), F32), sums[2 * hh]], axis=1)
            w_hi.append(weights(base[2 * hh], 0.0, causal_hi))
            w_lo.append(weights(base[2 * hh + 1], carry_hi, causal))
            carry.append(carry_hi + sums[2 * hh + 1])
        return w_hi, w_lo, carry

    def pair_units(j):
        units = []
        for hh in range(heads):
            units += [(j, qhs[hh], None), (j - 1, qhs[hh], None)]
        return units

    def pair_weights(base, sums, carry_in):
        ws, carry_out = [], []
        for hh in range(heads):
            w_near = weights(base[2 * hh], carry_in[hh], None)
            carry = carry_in[hh] + sums[2 * hh]
            w_far = weights(base[2 * hh + 1], carry, None)
            ws.append(jnp.concatenate([w_near, w_far], axis=0))
            carry_out.append(carry + sums[2 * hh + 1])
        return ws, carry_out

    def pair_values(j, hh):
        return jnp.concatenate([values(j, hh), values(j - 1, hh)], axis=1)

    def widen(acc_hi):
        return jnp.concatenate([jnp.zeros((SB_HEAD_DIM, tk), F32), acc_hi], axis=1)

    def finish(acc):
        o = jnp.concatenate(acc, axis=0).T
        o_ref[...] = (o * z_ref[...].astype(F32)).astype(BF16)

    @pl.when(i == 0)
    def _():
        base, sums = scores(diagonal_units())
        w_hi, w_lo, _ = diagonal_weights(base, sums)
        acc_hi = [_dot(values(j_hi, hh), w_hi[hh]) for hh in range(heads)]
        acc_lo = [_dot(values(j_lo, hh), w_lo[hh]) for hh in range(heads)]
        finish([lo + widen(hi) for lo, hi in zip(acc_lo, acc_hi)])

    @pl.when(i > 0)
    def _():
        n_diag = 2 * heads
        base, sums = scores(diagonal_units() + pair_units(2 * i - 1))
        w_hi, w_lo, carry = diagonal_weights(base[:n_diag], sums[:n_diag])
        w_pair, carry = pair_weights(base[n_diag:], sums[n_diag:], carry)
        acc_hi = [_dot(values(j_hi, hh), w_hi[hh]) for hh in range(heads)]
        acc_rest = [_dot(jnp.concatenate([values(j_lo, hh), pair_values(2 * i - 1, hh)], axis=1),
                         jnp.concatenate([w_lo[hh], w_pair[hh]], axis=0)) for hh in range(heads)]
        acc = [rest + widen(hi) for rest, hi in zip(acc_rest, acc_hi)]

        def least(carry):
            return jnp.min(functools.reduce(jnp.minimum, carry))

        def cond(st):
            return (st[0] >= 1) & (st[1] < EXIT_LOG2)

        def body(st):
            j, carry, acc = st[0], list(st[2:2 + heads]), list(st[2 + heads:])
            base, sums = scores(pair_units(j))
            ws, carry = pair_weights(base, sums, carry)
            acc = [a + _dot(pair_values(j, hh), ws[hh]) for hh, a in enumerate(acc)]
            return (j - 2, least(carry), *carry, *acc)

        final = lax.while_loop(cond, body, (2 * i - 3, least(carry), *carry, *acc))
        finish(list(final[2 + heads:]))


def _attention(q, k, vt, z, *, batch, seq):
    tq, tk = ATTN_Q, TR_BLOCK
    assert tq == 2 * tk and LANES == 2 * SB_HEAD_DIM
    n_q = seq // tq
    pairs = SB_WIDTH // LANES
    k3 = k.reshape(batch, seq, SB_WIDTH)
    qspec = pl.BlockSpec((tq, LANES), lambda b, p, i: (b * n_q + i, p))
    return pl.pallas_call(
        functools.partial(_attn_kernel, tq=tq, tk=tk),
        out_shape=jax.ShapeDtypeStruct(q.shape, BF16),
        grid=(batch, pairs, n_q),
        in_specs=[
            qspec,
            pl.BlockSpec((1, seq, LANES), lambda b, p, i: (b, 0, p)),
            pl.BlockSpec((1, seq // tk, LANES, tk), lambda b, p, i: (b, 0, p, 0)),
            qspec,
        ],
        out_specs=qspec,
        compiler_params=pltpu.CompilerParams(
            dimension_semantics=("arbitrary", "arbitrary", "arbitrary"), vmem_limit_bytes=VMEM_LIMIT),
        name="sb_attention",
    )(q, k3, vt, z)


def _inverse_minus_identity(ms, ri, ci, size):
    mm = lambda a, b: _dot(a.astype(BF16), b.astype(BF16))
    each = lambda f, *lists: [f(*xs) for xs in zip(*lists)]
    base = _block_of(ri, INV_BASE) == _block_of(ci, INV_BASE)
    md = each(lambda m: jnp.where(base, m, 0.0), ms)
    m2 = each(lambda a: mm(a, a), md)
    m4 = each(lambda a: mm(a, a), m2)
    m8 = each(lambda a: mm(a, a), m4)
    n = each(lambda a, b: b - a - mm(a, b), md, m2)
    n = each(lambda a, b: a + b + mm(a, b), n, m4)
    n = each(lambda a, b: a + b + mm(a, b), n, m8)
    width = INV_BASE
    while width < size:
        inner = _block_of(ri, width) == _block_of(ci, width)
        outer = _block_of(ri, 2 * width) == _block_of(ci, 2 * width)
        below = outer & jnp.logical_not(inner)
        mo = each(lambda m: jnp.where(below, m, 0.0), ms)
        x = each(lambda a, b: b + mm(a, b), n, mo)
        n = each(lambda a, b: a - b - mm(b, a), n, x)
        width *= 2
    return n


def _dn_kernel(q_ref, k_ref, kt_ref, v_ref, z_ref, gc_ref, gr_ref, ng_ref, o_ref, state_ref, *, blk):
    @pl.when(pl.program_id(1) == 0)
    def _():
        state_ref[...] = jnp.zeros_like(state_ref)

    ri = lax.broadcasted_iota(jnp.int32, (blk, blk), 0)
    ci = lax.broadcasted_iota(jnp.int32, (blk, blk), 1)
    tril = ci <= ri
    strict = ci < ri
    gates = gc_ref[...]
    grows = gr_ref[0]
    heads = range(DN_HEADS)
    each = lambda f, *lists: [f(*xs) for xs in zip(*lists)]
    cols = [slice(h * DN_HEAD_DIM, (h + 1) * DN_HEAD_DIM) for h in heads]
    q = [q_ref[:, c] for c in cols]
    k = [k_ref[:, c] for c in cols]
    beta = [gates[:, h:h + 1] for h in heads]
    g = [gates[:, DN_HEADS + h:DN_HEADS + h + 1] for h in heads]
    g_row = [grows[DN_HEADS + h:DN_HEADS + h + 1, :] for h in heads]
    gt_row = [grows[2 * DN_HEADS + h:2 * DN_HEADS + h + 1, :] for h in heads]
    eg = each(jnp.exp, g)
    kb = each(lambda a, b: a.astype(F32) * b, k, beta)
    vb = [v_ref[:, c].astype(F32) * b for c, b in zip(cols, beta)]
    decay = each(lambda a, b: jnp.exp(jnp.where(tril, a - b, NEG_BIG)), g, g_row)
    m = each(lambda a, b, d: jnp.where(strict, _dot_nt(a.astype(BF16), b) * d, 0.0), kb, k, decay)
    n = each(lambda a: a.astype(BF16), _inverse_minus_identity(m, ri, ci, blk))
    uw = each(lambda a, b, e: jnp.concatenate([a, b * e], axis=1), vb, kb, eg)
    uw = each(lambda a, b: b + _dot(a, b.astype(BF16)), n, uw)
    intra = each(lambda a, b, d: (_dot_nt(a, b) * d).astype(BF16), q, k, decay)

    state = [state_ref[h] for h in heads]
    wq = each(lambda a, b, e: jnp.concatenate([a[:, DN_HEAD_DIM:], b.astype(F32) * e], axis=0).astype(BF16),
              uw, q, eg)
    ws = each(lambda a, s: _dot(a, s.astype(BF16)), wq, state)
    v_new = each(lambda a, b: (a[:, :DN_HEAD_DIM] - b[:blk]).astype(BF16), uw, ws)
    o = each(lambda a, b, c: a[blk:] + _dot(b, c), ws, intra, v_new)
    for h in heads:
        kt = jnp.concatenate([kt_ref[0, a, cols[h], :] for a in range(blk // TR_BLOCK)], axis=1)
        ktd = (kt.astype(F32) * jnp.exp(gt_row[h] - g_row[h])).astype(BF16)
        state_ref[h] = state[h] * jnp.exp(gt_row[h][:, 0:1]) + _dot(ktd, v_new[h])
    for h in heads:
        ms = jnp.mean(o[h] * o[h], axis=-1, keepdims=True)
        on = o[h] * lax.rsqrt(ms + EPS) * ng_ref[...]
        o_ref[:, cols[h]] = (on * z_ref[:, cols[h]].astype(F32)).astype(BF16)


def _deltanet(dq, dk, dkt, dv, dz, gc, gr, ng, *, batch, seq):
    blk = DN_BLOCK
    n_blk = seq // blk
    tok = pl.BlockSpec((blk, DN_WIDTH), lambda b, c: (b * n_blk + c, 0))
    return pl.pallas_call(
        functools.partial(_dn_kernel, blk=blk),
        out_shape=jax.ShapeDtypeStruct(dq.shape, BF16),
        grid=(batch, n_blk),
        in_specs=[
            tok, tok,
            pl.BlockSpec((1, blk // TR_BLOCK, DN_WIDTH, TR_BLOCK), lambda b, c: (b, c, 0, 0)),
            tok, tok,
            pl.BlockSpec((blk, LANES), lambda b, c: (b * n_blk + c, 0)),
            pl.BlockSpec((1, GATE_ROWS, blk), lambda b, c: (b, 0, c)),
            pl.BlockSpec((1, DN_HEAD_DIM), lambda b, c: (0, 0)),
        ],
        out_specs=tok,
        scratch_shapes=[pltpu.VMEM((DN_HEADS, DN_HEAD_DIM, DN_HEAD_DIM), F32)],
        compiler_params=pltpu.CompilerParams(
            dimension_semantics=("arbitrary", "arbitrary"), vmem_limit_bytes=VMEM_LIMIT),
        name="deltanet",
    )(dq, dk, dkt, dv, dz, gc, gr, ng)


def _outproj_kernel(x_ref, osb_ref, odn_ref, msb_ref, mdn_ref, gate_ref, wsb_ref, wdn_ref, wo_ref, o_ref):
    ysb = _dot(osb_ref[...], wsb_ref[...])
    ydn = _dot(odn_ref[...], wdn_ref[...])
    y = msb_ref[...].astype(F32) * ysb + mdn_ref[...].astype(F32) * ydn
    out = _dot(y.astype(BF16), wo_ref[...])
    o_ref[...] = x_ref[...] + gate_ref[0] * out


def _outproj(x2, osb, odn, msb, mdn, gate, wsb, wdn, wo, *, seq):
    n, d = x2.shape
    tm = TOKEN_TILE
    s_tiles = seq // tm
    tile = lambda w: pl.BlockSpec((tm, w), lambda t: (t, 0))
    full = lambda a: pl.BlockSpec(a.shape, lambda t: (0,) * a.ndim)
    return pl.pallas_call(
        _outproj_kernel,
        out_shape=jax.ShapeDtypeStruct((n, d), F32),
        grid=(n // tm,),
        in_specs=[tile(d), tile(SB_WIDTH), tile(DN_WIDTH), tile(d), tile(d),
                  pl.BlockSpec((1, 1, d), lambda t: (t // s_tiles, 0, 0)),
                  full(wsb), full(wdn), full(wo)],
        out_specs=tile(d),
        compiler_params=pltpu.CompilerParams(
            dimension_semantics=("arbitrary",), vmem_limit_bytes=VMEM_LIMIT),
        name="outproj",
    )(x2, osb, odn, msb, mdn, gate, wsb, wdn, wo)


def _layer(x2, mod, norm_g, w_in, sb_q_g, sb_k_g, conv_w, dn_a_log, dn_dt_bias, dn_norm_g,
           w_branch_sb, w_branch_dn, w_out, *, batch, seq):
    d = x2.shape[1]
    shift = mod[:batch, 0:d].reshape(batch, 1, d)
    scale = mod[:batch, d:2 * d].reshape(batch, 1, d)
    gate = mod[:batch, 2 * d:3 * d].reshape(batch, 1, d)

    main_cols = 4 * SB_WIDTH + 4 * DN_WIDTH
    wm = w_in[:, :main_cols].astype(BF16)
    w_b = w_in[:, main_cols:main_cols + DN_HEADS]
    w_a = w_in[:, main_cols + DN_HEADS:main_cols + 2 * DN_HEADS]
    pad = jnp.zeros((d, LANES - 3 * DN_HEADS), F32)
    wba = jnp.concatenate([w_b, w_a, w_a, pad], axis=1).astype(BF16)
    wmg = w_in[:, main_cols + 2 * DN_HEADS:].astype(BF16)

    def gate_lanes(v):
        z = jnp.zeros((DN_HEADS,), F32)
        return jnp.concatenate([z, v, v, jnp.zeros((LANES - 3 * DN_HEADS,), F32)]).reshape(1, LANES)

    gq = (jnp.tile(sb_q_g, SB_HEADS) * (LOG2E * SB_HEAD_DIM ** -0.5)).reshape(1, SB_WIDTH)
    gk = jnp.tile(sb_k_g, SB_HEADS).reshape(1, SB_WIDTH)
    head_of = jnp.arange(SB_WIDTH) // SB_HEAD_DIM
    grp = jnp.where(head_of[:, None] == head_of[None, :], 1.0 / SB_HEAD_DIM, 0.0).astype(BF16)

    (q, k, vt, z, dq, dk, dkt, dv, dz, gc, gr, msb, mdn) = _inproj(
        x2, scale, shift, norm_g.reshape(1, d), wm, wba, wmg, gq, gk, grp, conv_w,
        gate_lanes(dn_a_log), gate_lanes(dn_dt_bias), batch=batch, seq=seq)
    osb = _attention(q, k, vt, z, batch=batch, seq=seq)
    odn = _deltanet(dq, dk, dkt, dv, dz, gc, gr, dn_norm_g.reshape(1, DN_HEAD_DIM), batch=batch, seq=seq)
    return _outproj(x2, osb, odn, msb, mdn, gate, w_branch_sb.astype(BF16), w_branch_dn.astype(BF16),
                    w_out.astype(BF16), seq=seq)


def kernel(x, c, ada_w, ada_b, norm_g, w_in, sb_q_g, sb_k_g, conv_w, dn_a_log, dn_dt_bias, dn_norm_g,
           w_branch_sb, w_branch_dn, w_out):
    batch, seq, d = x.shape
    depth = ada_w.shape[0]
    assert seq % TOKEN_TILE == 0 and TOKEN_TILE % DN_BLOCK == 0 and TOKEN_TILE % ATTN_Q == 0
    rows = 16
    c_pad = jnp.concatenate([c, jnp.zeros((rows - batch, d), c.dtype)], axis=0) if batch < rows else c
    mod = _adaln(c_pad, ada_w, ada_b)
    x2 = x.reshape(batch * seq, d)
    for l in range(depth):
        x2 = _layer(x2, mod[l], norm_g[l], w_in[l], sb_q_g[l], sb_k_g[l], conv_w[l], dn_a_log[l],
                    dn_dt_bias[l], dn_norm_g[l], w_branch_sb[l], w_branch_dn[l], w_out[l],
                    batch=batch, seq=seq)
    return x2.reshape(batch, seq, d)
```

```python
import functools

import jax
import jax.numpy as jnp
from jax import lax
from jax.experimental import pallas as pl
from jax.experimental.pallas import tpu as pltpu

F32 = jnp.float32
BF16 = jnp.bfloat16

EPS = 1e-6
SB_HEADS = 8
SB_HEAD_DIM = 64
SB_WIDTH = SB_HEADS * SB_HEAD_DIM
DN_HEADS = 4
DN_HEAD_DIM = 128
DN_WIDTH = DN_HEADS * DN_HEAD_DIM
CONV_WIDTH = 4

LANES = 128
HALO_ROWS = 8
GATE_ROWS = 16
TOKEN_TILE = 512
ATTN_Q = 256
ATTN_HEADS = 4
TR_BLOCK = 128
DN_BLOCK = 512
DN_CHUNK = 128
INV_BASE = 16
VMEM_LIMIT = 56 * 1024 * 1024
NEG_BIG = -1e30
LOG2E = 1.4426950408889634
EXIT_LOG2 = 152.0


def _dot(a, b):
    return jnp.dot(a, b, preferred_element_type=F32)


def _dot_nt(a, b):
    return lax.dot_general(a, b, (((1,), (1,)), ((), ())), preferred_element_type=F32)


def _sigmoid(x):
    return 0.5 * jnp.tanh(0.5 * x) + 0.5


def _silu(x):
    h = 0.5 * x
    return h + h * jnp.tanh(h)


def _neg_abs(x):
    bits = lax.bitcast_convert_type(x, jnp.uint32) | jnp.uint32(0x80000000)
    return lax.bitcast_convert_type(bits, F32)


def _softplus(x):
    return jnp.maximum(x, 0.0) + jnp.log1p(jnp.exp(-jnp.abs(x)))


def _block_of(idx, size):
    assert size & (size - 1) == 0
    return jnp.right_shift(idx, size.bit_length() - 1)


def _split3(x):
    hi = x.astype(BF16)
    r = x - hi.astype(F32)
    mid = r.astype(BF16)
    lo = (r - mid.astype(F32)).astype(BF16)
    return hi, mid, lo


def _adaln_kernel(c_ref, w_ref, b_ref, o_ref):
    c = c_ref[...]
    a = c * _sigmoid(c)
    w = w_ref[0]
    a_hi = a.astype(BF16)
    a_lo = (a - a_hi.astype(F32)).astype(BF16)
    w_hi = w.astype(BF16)
    w_lo = (w - w_hi.astype(F32)).astype(BF16)
    o_ref[0] = _dot(a_hi, w_hi) + _dot(a_hi, w_lo) + _dot(a_lo, w_hi) + b_ref[0]


def _adaln(c_pad, ada_w, ada_b):
    depth, d, d3 = ada_w.shape
    rows = c_pad.shape[0]
    tn = 512
    return pl.pallas_call(
        _adaln_kernel,
        out_shape=jax.ShapeDtypeStruct((depth, rows, d3), F32),
        grid=(depth, d3 // tn),
        in_specs=[
            pl.BlockSpec((rows, d), lambda l, j: (0, 0)),
            pl.BlockSpec((1, d, tn), lambda l, j: (l, 0, j)),
            pl.BlockSpec((1, 1, tn), lambda l, j: (l, 0, j)),
        ],
        out_specs=pl.BlockSpec((1, rows, tn), lambda l, j: (l, 0, j)),
        compiler_params=pltpu.CompilerParams(
            dimension_semantics=("arbitrary", "arbitrary"), vmem_limit_bytes=VMEM_LIMIT),
        name="adaln",
    )(c_pad, ada_w, ada_b.reshape(depth, 1, d3))


def _inproj_kernel(x_ref, xh_ref, sc_ref, sh_ref, ng_ref, wm_ref, wba_ref, wmg_ref, gq_ref, gk_ref,
                   grp_ref, cw_ref, alog_ref, dtb_ref,
                   q_ref, k_ref, vt_ref, z_ref, dq_ref, dk_ref, dkt_ref, dv_ref, dz_ref,
                   gc_ref, gr_ref, msb_ref, mdn_ref, plain_ref, conv_ref, ba_ref, mg_ref, *, tm, s_tiles):
    t = pl.program_id(0)
    gmod = ng_ref[...] * (1.0 + sc_ref[0])
    shift = sh_ref[0]

    def norm_mod(xv):
        ms = jnp.mean(xv * xv, axis=-1, keepdims=True)
        return (xv * lax.rsqrt(ms + EPS) * gmod + shift).astype(BF16)

    hb = norm_mod(x_ref[...])
    hh = norm_mod(xh_ref[...])
    seq_start = (t % s_tiles) == 0
    cw = cw_ref[...]
    d_model = msb_ref.shape[1]

    def main(g):
        return wm_ref[:, g * SB_WIDTH:(g + 1) * SB_WIDTH]

    for slot, g in enumerate((0, 1, 2, 3, 7)):
        plain_ref[slot] = _dot(hb, main(g))
    for slot in range(3):
        conv_ref[slot, HALO_ROWS:, :] = _dot(hb, main(4 + slot))
        conv_ref[slot, 0:HALO_ROWS, :] = jnp.where(seq_start, 0.0, _dot(hh, main(4 + slot)))
    ba_ref[...] = _dot(hb, wba_ref[...])
    for slot in range(2):
        mg_ref[slot] = _dot(hb, wmg_ref[:, slot * d_model:(slot + 1) * d_model])

    def head_rms(p, gain):
        ms = _dot((p * p).astype(BF16), grp_ref[...])
        return p * lax.rsqrt(ms + EPS) * gain

    def store_transposed(ref, val):
        for a in range(tm // TR_BLOCK):
            ref[0, a] = val[a * TR_BLOCK:(a + 1) * TR_BLOCK, :].T.astype(BF16)

    def conv_silu(slot):
        buf = conv_ref.at[slot]
        taps = cw[:, slot * DN_WIDTH:(slot + 1) * DN_WIDTH]
        acc = buf[HALO_ROWS:, :] * taps[CONV_WIDTH - 1:CONV_WIDTH, :]
        for back in range(1, CONV_WIDTH):
            src = buf[HALO_ROWS - back:HALO_ROWS - back + tm, :]
            acc = acc + src * taps[CONV_WIDTH - 1 - back:CONV_WIDTH - back, :]
        return _silu(acc)

    def l2n(y, scale):
        outs = []
        for h in range(DN_HEADS):
            yh = y[:, h * DN_HEAD_DIM:(h + 1) * DN_HEAD_DIM]
            ss = jnp.sum(yh * yh, axis=-1, keepdims=True)
            outs.append(yh * (lax.rsqrt(ss + EPS) * scale))
        return jnp.concatenate(outs, axis=1)

    def gates():
        pba = ba_ref[...]
        beta = _sigmoid(pba)
        ld = -jnp.exp(alog_ref[...]) * _softplus(pba + dtb_ref[...])
        lane = lax.broadcasted_iota(jnp.int32, (tm, LANES), 1)
        ld = jnp.where((lane >= DN_HEADS) & (lane < 3 * DN_HEADS), ld, 0.0)
        ri = lax.broadcasted_iota(jnp.int32, (tm, tm), 0)
        ci = lax.broadcasted_iota(jnp.int32, (tm, tm), 1)
        same = _block_of(ri, DN_CHUNK) == _block_of(ci, DN_CHUNK)
        lower = jnp.where(same & (ci <= ri), 1.0, 0.0).astype(BF16)
        whole = jnp.where(same, 1.0, 0.0).astype(BF16)
        parts = _split3(ld)
        gcum = _dot(lower, parts[0]) + _dot(lower, parts[1]) + _dot(lower, parts[2])
        gtot = _dot(whole, parts[0]) + _dot(whole, parts[1]) + _dot(whole, parts[2])
        g = jnp.where(lane < DN_HEADS, beta, jnp.where(lane < 2 * DN_HEADS, gcum, gtot))
        gc_ref[...] = g
        gr_ref[0] = g.T[0:GATE_ROWS, :]

    q_ref[...] = head_rms(plain_ref[0], gq_ref[...]).astype(BF16)
    k_ref[...] = head_rms(plain_ref[1], gk_ref[...]).astype(BF16)
    store_transposed(vt_ref, plain_ref[2])
    z_ref[...] = _silu(plain_ref[3]).astype(BF16)
    dz_ref[...] = _silu(plain_ref[4]).astype(BF16)
    dq_ref[...] = l2n(conv_silu(0), DN_HEAD_DIM ** -0.5).astype(BF16)
    yk = l2n(conv_silu(1), 1.0)
    dk_ref[...] = yk.astype(BF16)
    store_transposed(dkt_ref, yk)
    dv_ref[...] = conv_silu(2).astype(BF16)
    gates()
    msb_ref[...] = _sigmoid(mg_ref[0]).astype(BF16)
    mdn_ref[...] = _sigmoid(mg_ref[1]).astype(BF16)


def _inproj(x2, scale, shift, norm_g, wm, wba, wmg, gq, gk, grp, cw, alog, dtb, *, batch, seq):
    n, d = x2.shape
    tm = TOKEN_TILE
    s_tiles = seq // tm
    n_sub = tm // TR_BLOCK
    n_blk = seq // TR_BLOCK
    tile = lambda w: pl.BlockSpec((tm, w), lambda t: (t, 0))
    full = lambda a: pl.BlockSpec(a.shape, lambda t: (0,) * a.ndim, pipeline_mode=pl.Buffered(1))
    per_batch = pl.BlockSpec((1, 1, d), lambda t: (t // s_tiles, 0, 0))
    tr_spec = pl.BlockSpec((1, n_sub, SB_WIDTH, TR_BLOCK), lambda t: (t // s_tiles, t % s_tiles, 0, 0))
    tr_shape = jax.ShapeDtypeStruct((batch, n_blk, SB_WIDTH, TR_BLOCK), BF16)
    tok = lambda w, dt=BF16: jax.ShapeDtypeStruct((n, w), dt)
    out_shape = (
        tok(SB_WIDTH), tok(SB_WIDTH), tr_shape, tok(SB_WIDTH),
        tok(DN_WIDTH), tok(DN_WIDTH), tr_shape, tok(DN_WIDTH), tok(DN_WIDTH),
        tok(LANES, F32),
        jax.ShapeDtypeStruct((batch, GATE_ROWS, seq), F32),
        tok(d), tok(d),
    )
    out_specs = (
        tile(SB_WIDTH), tile(SB_WIDTH), tr_spec, tile(SB_WIDTH),
        tile(DN_WIDTH), tile(DN_WIDTH), tr_spec, tile(DN_WIDTH), tile(DN_WIDTH),
        tile(LANES),
        pl.BlockSpec((1, GATE_ROWS, tm), lambda t: (t // s_tiles, 0, t % s_tiles)),
        tile(d), tile(d),
    )
    halo_blocks = tm // HALO_ROWS
    in_specs = [
        tile(d),
        pl.BlockSpec((HALO_ROWS, d), lambda t: (jnp.maximum(t * halo_blocks - 1, 0), 0)),
        per_batch, per_batch, full(norm_g), full(wm), full(wba), full(wmg), full(gq), full(gk),
        full(grp), full(cw), full(alog), full(dtb),
    ]
    return pl.pallas_call(
        functools.partial(_inproj_kernel, tm=tm, s_tiles=s_tiles),
        out_shape=out_shape,
        grid=(n // tm,),
        in_specs=in_specs,
        out_specs=out_specs,
        scratch_shapes=[pltpu.VMEM((5, tm, SB_WIDTH), F32),
                        pltpu.VMEM((3, HALO_ROWS + tm, DN_WIDTH), F32),
                        pltpu.VMEM((tm, LANES), F32),
                        pltpu.VMEM((2, tm, d), F32)],
        compiler_params=pltpu.CompilerParams(
            dimension_semantics=("arbitrary",), vmem_limit_bytes=VMEM_LIMIT),
        name="inproj",
    )(x2, x2, scale, shift, norm_g, wm, wba, wmg, gq, gk, grp, cw, alog, dtb)


def _attn_kernel(q_ref, k_ref, vt_ref, z_ref, o_ref, *, tq, tk):
    i = pl.program_id(2)
    width = q_ref.shape[1]
    heads = width // SB_HEAD_DIM
    ks = lax.broadcasted_iota(jnp.int32, (tk, tq), 0)
    qt = lax.broadcasted_iota(jnp.int32, (tk, tq), 1)
    causal = ks < qt
    ur = lax.broadcasted_iota(jnp.int32, (tk, tk), 0)
    uc = lax.broadcasted_iota(jnp.int32, (tk, tk), 1)
    upper = jnp.where(uc > ur, 1.0, 0.0).astype(BF16)
    lane = lax.broadcasted_iota(jnp.int32, (tq, width), 1)
    q = q_ref[...]
    qhs = [jnp.where(_block_of(lane, SB_HEAD_DIM) == hh, q, jnp.zeros_like(q)) for hh in range(heads)]

    def scores(units):
        st = [_dot_nt(k_ref[0, pl.ds(pl.multiple_of(j * tk, tk), tk), :], qh) for j, qh, _ in units]
        sp = [jnp.maximum(s, 0.0) + jnp.log(1.0 + jnp.exp2(_neg_abs(s))) * LOG2E for s in st]
        spm = [p if u[2] is None else jnp.where(u[2], p, 0.0) for p, u in zip(sp, units)]
        later = [_dot(upper, m.astype(BF16)) for m in spm]
        base = [s - p - l for s, p, l in zip(st, sp, later)]
        sums = [l[0:1, :] + m[0:1, :] for l, m in zip(later, spm)]
        return base, sums

    def weights(base, carry, mask):
        w = jnp.exp2(base - carry)
        if mask is not None:
            w = jnp.where(mask, w, 0.0)
        return w.astype(BF16)

    def values(j, hh):
        return vt_ref[0, j, hh * SB_HEAD_DIM:(hh + 1) * SB_HEAD_DIM, :]

    j_hi = 2 * i + 1
    j_lo = 2 * i
    causal_hi = causal[:, :tk]

    def diagonal_units():
        units = []
        for hh in range(heads):
            units += [(j_hi, qhs[hh][tk:, :], causal_hi), (j_lo, qhs[hh], causal)]
        return units

    def diagonal_weights(base, sums):
        w_hi, w_lo, carry = [], [], []
        for hh in range(heads):
            carry_hi = jnp.concatenate([jnp.zeros((1, tk), F32), sums[2 * hh]], axis=1)
            w_hi.append(weights(base[2 * hh], 0.0, causal_hi))
            w_lo.append(weights(base[2 * hh + 1], carry_hi, causal))
            carry.append(carry_hi + sums[2 * hh + 1])
        return w_hi, w_lo, carry

    def pair_units(j):
        units = []
        for hh in range(heads):
            units += [(j, qhs[hh], None), (j - 1, qhs[hh], None)]
        return units

    def pair_weights(base, sums, carry_in):
        ws, carry_out = [], []
        for hh in range(heads):
            w_near = weights(base[2 * hh], carry_in[hh], None)
            carry = carry_in[hh] + sums[2 * hh]
            w_far = weights(base[2 * hh + 1], carry, None)
            ws.append(jnp.concatenate([w_near, w_far], axis=0))
            carry_out.append(carry + sums[2 * hh + 1])
        return ws, carry_out

    def pair_values(j, hh):
        return jnp.concatenate([values(j, hh), values(j - 1, hh)], axis=1)

    def widen(acc_hi):
        return jnp.concatenate([jnp.zeros((SB_HEAD_DIM, tk), F32), acc_hi], axis=1)

    def finish(acc):
        o = jnp.concatenate(acc, axis=0).T
        o_ref[...] = (o * z_ref[...].astype(F32)).astype(BF16)

    @pl.when(i == 0)
    def _():
        base, sums = scores(diagonal_units())
        w_hi, w_lo, _ = diagonal_weights(base, sums)
        acc_hi = [_dot(values(j_hi, hh), w_hi[hh]) for hh in range(heads)]
        acc_lo = [_dot(values(j_lo, hh), w_lo[hh]) for hh in range(heads)]
        finish([lo + widen(hi) for lo, hi in zip(acc_lo, acc_hi)])

    @pl.when(i > 0)
    def _():
        n_diag = 2 * heads
        base, sums = scores(diagonal_units() + pair_units(2 * i - 1))
        w_hi, w_lo, carry = diagonal_weights(base[:n_diag], sums[:n_diag])
        w_pair, carry = pair_weights(base[n_diag:], sums[n_diag:], carry)
        acc_hi = [_dot(values(j_hi, hh), w_hi[hh]) for hh in range(heads)]
        acc_rest = [_dot(jnp.concatenate([values(j_lo, hh), pair_values(2 * i - 1, hh)], axis=1),
                         jnp.concatenate([w_lo[hh], w_pair[hh]], axis=0)) for hh in range(heads)]
        acc = [rest + widen(hi) for rest, hi in zip(acc_rest, acc_hi)]

        def least(carry):
            return jnp.min(functools.reduce(jnp.minimum, carry))

        def cond(st):
            return (st[0] >= 1) & (st[1] < EXIT_LOG2)

        def body(st):
            j, carry, acc = st[0], list(st[2:2 + heads]), list(st[2 + heads:])
            base, sums = scores(pair_units(j))
            ws, carry = pair_weights(base, sums, carry)
            acc = [a + _dot(pair_values(j, hh), ws[hh]) for hh, a in enumerate(acc)]
            return (j - 2, least(carry), *carry, *acc)

        final = lax.while_loop(cond, body, (2 * i - 3, least(carry), *carry, *acc))
        finish(list(final[2 + heads:]))


def _attention(q, k, vt, z, *, batch, seq):
    tq, tk = ATTN_Q, TR_BLOCK
    width = ATTN_HEADS * SB_HEAD_DIM
    assert tq == 2 * tk and width % LANES == 0 and SB_WIDTH % width == 0
    n_q = seq // tq
    groups = SB_WIDTH // width
    k3 = k.reshape(batch, seq, SB_WIDTH)
    qspec = pl.BlockSpec((tq, width), lambda b, p, i: (b * n_q + i, p))
    return pl.pallas_call(
        functools.partial(_attn_kernel, tq=tq, tk=tk),
        out_shape=jax.ShapeDtypeStruct(q.shape, BF16),
        grid=(batch, groups, n_q),
        in_specs=[
            qspec,
            pl.BlockSpec((1, seq, width), lambda b, p, i: (b, 0, p)),
            pl.BlockSpec((1, seq // tk, width, tk), lambda b, p, i: (b, 0, p, 0)),
            qspec,
        ],
        out_specs=qspec,
        compiler_params=pltpu.CompilerParams(
            dimension_semantics=("arbitrary", "arbitrary", "arbitrary"), vmem_limit_bytes=VMEM_LIMIT),
        name="sb_attention",
    )(q, k3, vt, z)


def _inverse_minus_identity(ms, ri, ci, size):
    mm = lambda a, b: _dot(a.astype(BF16), b.astype(BF16))
    each = lambda f, *lists: [f(*xs) for xs in zip(*lists)]
    base = _block_of(ri, INV_BASE) == _block_of(ci, INV_BASE)
    md = each(lambda m: jnp.where(base, m, 0.0), ms)
    m2 = each(lambda a: mm(a, a), md)
    m4 = each(lambda a: mm(a, a), m2)
    m8 = each(lambda a: mm(a, a), m4)
    n = each(lambda a, b: b - a - mm(a, b), md, m2)
    n = each(lambda a, b: a + b + mm(a, b), n, m4)
    n = each(lambda a, b: a + b + mm(a, b), n, m8)
    width = INV_BASE
    while width < size:
        inner = _block_of(ri, width) == _block_of(ci, width)
        outer = _block_of(ri, 2 * width) == _block_of(ci, 2 * width)
        below = outer & jnp.logical_not(inner)
        mo = each(lambda m: jnp.where(below, m, 0.0), ms)
        x = each(lambda a, b: b + mm(a, b), n, mo)
        n = each(lambda a, b: a - b - mm(b, a), n, x)
        width *= 2
    return n


def _dn_kernel(q_ref, k_ref, kt_ref, v_ref, z_ref, gc_ref, gr_ref, ng_ref, o_ref, state_ref, *, blk):
    @pl.when(pl.program_id(1) == 0)
    def _():
        state_ref[...] = jnp.zeros_like(state_ref)

    cs = DN_CHUNK
    ri = lax.broadcasted_iota(jnp.int32, (cs, cs), 0)
    ci = lax.broadcasted_iota(jnp.int32, (cs, cs), 1)
    tril = ci <= ri
    strict = ci < ri
    gates = gc_ref[...]
    grows = gr_ref[0]
    each = lambda f, *lists: [f(*xs) for xs in zip(*lists)]
    units = [(c, h) for c in range(blk // cs) for h in range(DN_HEADS)]
    rows = lambda c: slice(c * cs, (c + 1) * cs)
    cols = lambda h: slice(h * DN_HEAD_DIM, (h + 1) * DN_HEAD_DIM)
    q = [q_ref[rows(c), cols(h)] for c, h in units]
    k = [k_ref[rows(c), cols(h)] for c, h in units]
    beta = [gates[rows(c), h:h + 1] for c, h in units]
    g = [gates[rows(c), DN_HEADS + h:DN_HEADS + h + 1] for c, h in units]
    g_row = [grows[DN_HEADS + h:DN_HEADS + h + 1, rows(c)] for c, h in units]
    gt_row = [grows[2 * DN_HEADS + h:2 * DN_HEADS + h + 1, rows(c)] for c, h in units]
    eg = each(jnp.exp, g)
    kb = each(lambda a, b: a.astype(F32) * b, k, beta)
    vb = [v_ref[rows(c), cols(h)].astype(F32) * b for (c, h), b in zip(units, beta)]
    decay = each(lambda a, b: jnp.exp(jnp.where(tril, a - b, NEG_BIG)), g, g_row)
    m = each(lambda a, b, d: jnp.where(strict, _dot_nt(a.astype(BF16), b) * d, 0.0), kb, k, decay)
    n = each(lambda a: a.astype(BF16), _inverse_minus_identity(m, ri, ci, cs))
    uw = each(lambda a, b, e: jnp.concatenate([a, b * e], axis=1), vb, kb, eg)
    uw = each(lambda a, b: b + _dot(a, b.astype(BF16)), n, uw)
    intra = each(lambda a, b, d: (_dot_nt(a, b) * d).astype(BF16), q, k, decay)
    wq = each(lambda a, b, e: jnp.concatenate([a[:, DN_HEAD_DIM:], b.astype(F32) * e], axis=0).astype(BF16),
              uw, q, eg)
    ktd = [(kt_ref[0, c, cols(h), :].astype(F32) * jnp.exp(t - r)).astype(BF16)
           for (c, h), t, r in zip(units, gt_row, g_row)]
    carry = each(lambda t: jnp.exp(t[:, 0:1]), gt_row)

    state = [state_ref[h] for h in range(DN_HEADS)]
    outs = []
    for c in range(blk // cs):
        of = lambda xs: xs[c * DN_HEADS:(c + 1) * DN_HEADS]
        ws = each(lambda a, s: _dot(a, s.astype(BF16)), of(wq), state)
        v_new = each(lambda a, b: (a[:, :DN_HEAD_DIM] - b[:cs]).astype(BF16), of(uw), ws)
        outs.append(each(lambda a, b, d: a[cs:] + _dot(b, d), ws, of(intra), v_new))
        state = each(lambda s, e, a, b: s * e + _dot(a, b), state, of(carry), of(ktd), v_new)
    for h in range(DN_HEADS):
        state_ref[h] = state[h]
        o = jnp.concatenate([chunk_out[h] for chunk_out in outs], axis=0)
        ms = jnp.mean(o * o, axis=-1, keepdims=True)
        on = o * lax.rsqrt(ms + EPS) * ng_ref[...]
        o_ref[:, cols(h)] = (on * z_ref[:, cols(h)].astype(F32)).astype(BF16)


def _deltanet(dq, dk, dkt, dv, dz, gc, gr, ng, *, batch, seq):
    blk = DN_BLOCK
    n_blk = seq // blk
    tok = pl.BlockSpec((blk, DN_WIDTH), lambda b, c: (b * n_blk + c, 0))
    return pl.pallas_call(
        functools.partial(_dn_kernel, blk=blk),
        out_shape=jax.ShapeDtypeStruct(dq.shape, BF16),
        grid=(batch, n_blk),
        in_specs=[
            tok, tok,
            pl.BlockSpec((1, blk // TR_BLOCK, DN_WIDTH, TR_BLOCK), lambda b, c: (b, c, 0, 0)),
            tok, tok,
            pl.BlockSpec((blk, LANES), lambda b, c: (b * n_blk + c, 0)),
            pl.BlockSpec((1, GATE_ROWS, blk), lambda b, c: (b, 0, c)),
            pl.BlockSpec((1, DN_HEAD_DIM), lambda b, c: (0, 0)),
        ],
        out_specs=tok,
        scratch_shapes=[pltpu.VMEM((DN_HEADS, DN_HEAD_DIM, DN_HEAD_DIM), F32)],
        compiler_params=pltpu.CompilerParams(
            dimension_semantics=("arbitrary", "arbitrary"), vmem_limit_bytes=VMEM_LIMIT),
        name="deltanet",
    )(dq, dk, dkt, dv, dz, gc, gr, ng)


def _outproj_kernel(x_ref, osb_ref, odn_ref, msb_ref, mdn_ref, gate_ref, wsb_ref, wdn_ref, wo_ref, o_ref):
    ysb = _dot(osb_ref[...], wsb_ref[...])
    ydn = _dot(odn_ref[...], wdn_ref[...])
    y = msb_ref[...].astype(F32) * ysb + mdn_ref[...].astype(F32) * ydn
    out = _dot(y.astype(BF16), wo_ref[...])
    o_ref[...] = x_ref[...] + gate_ref[0] * out


def _outproj(x2, osb, odn, msb, mdn, gate, wsb, wdn, wo, *, seq):
    n, d = x2.shape
    tm = TOKEN_TILE
    s_tiles = seq // tm
    tile = lambda w: pl.BlockSpec((tm, w), lambda t: (t, 0))
    full = lambda a: pl.BlockSpec(a.shape, lambda t: (0,) * a.ndim)
    return pl.pallas_call(
        _outproj_kernel,
        out_shape=jax.ShapeDtypeStruct((n, d), F32),
        grid=(n // tm,),
        in_specs=[tile(d), tile(SB_WIDTH), tile(DN_WIDTH), tile(d), tile(d),
                  pl.BlockSpec((1, 1, d), lambda t: (t // s_tiles, 0, 0)),
                  full(wsb), full(wdn), full(wo)],
        out_specs=tile(d),
        compiler_params=pltpu.CompilerParams(
            dimension_semantics=("arbitrary",), vmem_limit_bytes=VMEM_LIMIT),
        name="outproj",
    )(x2, osb, odn, msb, mdn, gate, wsb, wdn, wo)


def _layer(x2, mod, norm_g, w_in, sb_q_g, sb_k_g, conv_w, dn_a_log, dn_dt_bias, dn_norm_g,
           w_branch_sb, w_branch_dn, w_out, *, batch, seq):
    d = x2.shape[1]
    shift = mod[:batch, 0:d].reshape(batch, 1, d)
    scale = mod[:batch, d:2 * d].reshape(batch, 1, d)
    gate = mod[:batch, 2 * d:3 * d].reshape(batch, 1, d)

    main_cols = 4 * SB_WIDTH + 4 * DN_WIDTH
    wm = w_in[:, :main_cols].astype(BF16)
    w_b = w_in[:, main_cols:main_cols + DN_HEADS]
    w_a = w_in[:, main_cols + DN_HEADS:main_cols + 2 * DN_HEADS]
    pad = jnp.zeros((d, LANES - 3 * DN_HEADS), F32)
    wba = jnp.concatenate([w_b, w_a, w_a, pad], axis=1).astype(BF16)
    wmg = w_in[:, main_cols + 2 * DN_HEADS:].astype(BF16)

    def gate_lanes(v):
        z = jnp.zeros((DN_HEADS,), F32)
        return jnp.concatenate([z, v, v, jnp.zeros((LANES - 3 * DN_HEADS,), F32)]).reshape(1, LANES)

    gq = (jnp.tile(sb_q_g, SB_HEADS) * (LOG2E * SB_HEAD_DIM ** -0.5)).reshape(1, SB_WIDTH)
    gk = jnp.tile(sb_k_g, SB_HEADS).reshape(1, SB_WIDTH)
    head_of = jnp.arange(SB_WIDTH) // SB_HEAD_DIM
    grp = jnp.where(head_of[:, None] == head_of[None, :], 1.0 / SB_HEAD_DIM, 0.0).astype(BF16)

    (q, k, vt, z, dq, dk, dkt, dv, dz, gc, gr, msb, mdn) = _inproj(
        x2, scale, shift, norm_g.reshape(1, d), wm, wba, wmg, gq, gk, grp, conv_w,
        gate_lanes(dn_a_log), gate_lanes(dn_dt_bias), batch=batch, seq=seq)
    osb = _attention(q, k, vt, z, batch=batch, seq=seq)
    odn = _deltanet(dq, dk, dkt, dv, dz, gc, gr, dn_norm_g.reshape(1, DN_HEAD_DIM), batch=batch, seq=seq)
    return _outproj(x2, osb, odn, msb, mdn, gate, w_branch_sb.astype(BF16), w_branch_dn.astype(BF16),
                    w_out.astype(BF16), seq=seq)


def kernel(x, c, ada_w, ada_b, norm_g, w_in, sb_q_g, sb_k_g, conv_w, dn_a_log, dn_dt_bias, dn_norm_g,
           w_branch_sb, w_branch_dn, w_out):
    batch, seq, d = x.shape
    depth = ada_w.shape[0]
    assert seq % TOKEN_TILE == 0 and TOKEN_TILE % DN_BLOCK == 0 and TOKEN_TILE % ATTN_Q == 0
    assert DN_CHUNK == TR_BLOCK and DN_BLOCK % DN_CHUNK == 0
    rows = 16
    c_pad = jnp.concatenate([c, jnp.zeros((rows - batch, d), c.dtype)], axis=0) if batch < rows else c
    mod = _adaln(c_pad, ada_w, ada_b)
    x2 = x.reshape(batch * seq, d)
    for l in range(depth):
        x2 = _layer(x2, mod[l], norm_g[l], w_in[l], sb_q_g[l], sb_k_g[l], conv_w[l], dn_a_log[l],
                    dn_dt_bias[l], dn_norm_g[l], w_branch_sb[l], w_branch_dn[l], w_out[l],
                    batch=batch, seq=seq)
    return x2.reshape(batch, seq, d)
```

```python
import functools

import jax
import jax.numpy as jnp
from jax import lax
from jax.experimental import pallas as pl
from jax.experimental.pallas import tpu as pltpu

F32 = jnp.float32
BF16 = jnp.bfloat16

EPS = 1e-6
SB_HEADS = 8
SB_HEAD_DIM = 64
SB_WIDTH = SB_HEADS * SB_HEAD_DIM
DN_HEADS = 4
DN_HEAD_DIM = 128
DN_WIDTH = DN_HEADS * DN_HEAD_DIM
CONV_WIDTH = 4

LANES = 128
F32_SUBLANES = 8
HALO_ROWS = 16
GATE_ROWS = 16
TOKEN_TILE = 512
PIPE_DEPTH = 3
ATTN_Q = 256
ATTN_HEADS = 4
TR_BLOCK = 128
DN_BLOCK = 512
DN_CHUNK = 128
INV_BASE = 16
VMEM_LIMIT = 56 * 1024 * 1024
NEG_BIG = -1e30
LOG2E = 1.4426950408889634
EXIT_LOG2 = 152.0


def _dot(a, b):
    return jnp.dot(a, b, preferred_element_type=F32)


def _dot_nt(a, b):
    return lax.dot_general(a, b, (((1,), (1,)), ((), ())), preferred_element_type=F32)


def _sigmoid(x):
    return 0.5 * jnp.tanh(0.5 * x) + 0.5


def _silu(x):
    h = 0.5 * x
    return h + h * jnp.tanh(h)


def _neg_abs(x):
    bits = lax.bitcast_convert_type(x, jnp.uint32) | jnp.uint32(0x80000000)
    return lax.bitcast_convert_type(bits, F32)


def _softplus(x):
    return jnp.maximum(x, 0.0) + jnp.log1p(jnp.exp(-jnp.abs(x)))


def _block_of(idx, size):
    assert size & (size - 1) == 0
    return jnp.right_shift(idx, size.bit_length() - 1)


def _split3(x):
    hi = x.astype(BF16)
    r = x - hi.astype(F32)
    mid = r.astype(BF16)
    lo = (r - mid.astype(F32)).astype(BF16)
    return hi, mid, lo


def _adaln_kernel(c_ref, w_ref, b_ref, o_ref):
    c = c_ref[...]
    a = c * _sigmoid(c)
    w = w_ref[0]
    a_hi = a.astype(BF16)
    a_lo = (a - a_hi.astype(F32)).astype(BF16)
    w_hi = w.astype(BF16)
    w_lo = (w - w_hi.astype(F32)).astype(BF16)
    o_ref[0] = _dot(a_hi, w_hi) + _dot(a_hi, w_lo) + _dot(a_lo, w_hi) + b_ref[0]


def _adaln(c_pad, ada_w, ada_b):
    depth, d, d3 = ada_w.shape
    rows = c_pad.shape[0]
    tn = 512
    return pl.pallas_call(
        _adaln_kernel,
        out_shape=jax.ShapeDtypeStruct((depth, rows, d3), F32),
        grid=(depth, d3 // tn),
        in_specs=[
            pl.BlockSpec((rows, d), lambda l, j: (0, 0)),
            pl.BlockSpec((1, d, tn), lambda l, j: (l, 0, j)),
            pl.BlockSpec((1, 1, tn), lambda l, j: (l, 0, j)),
        ],
        out_specs=pl.BlockSpec((1, rows, tn), lambda l, j: (l, 0, j)),
        compiler_params=pltpu.CompilerParams(
            dimension_semantics=("arbitrary", "arbitrary"), vmem_limit_bytes=VMEM_LIMIT),
        name="adaln",
    )(c_pad, ada_w, ada_b.reshape(depth, 1, d3))


def _inproj_kernel(x_ref, xh_ref, sc_ref, sh_ref, ng_ref, wm_ref, wba_ref, wmg_ref, gq_ref, gk_ref,
                   grp_ref, cw_ref, alog_ref, dtb_ref,
                   q_ref, k_ref, vt_ref, z_ref, dq_ref, dk_ref, dkt_ref, dv_ref, dz_ref,
                   gc_ref, gr_ref, msb_ref, mdn_ref, *, tm, s_tiles):
    t = pl.program_id(0)
    gmod = ng_ref[...] * (1.0 + sc_ref[0])
    shift = sh_ref[0]

    def norm_mod(xv):
        ms = jnp.mean(xv * xv, axis=-1, keepdims=True)
        return (xv * lax.rsqrt(ms + EPS) * gmod + shift).astype(BF16)

    hb = norm_mod(x_ref[...])
    hh = norm_mod(xh_ref[...])
    seq_start = (t % s_tiles) == 0
    cw = cw_ref[...]
    d_model = msb_ref.shape[1]

    def main(g):
        return wm_ref[:, g * SB_WIDTH:(g + 1) * SB_WIDTH]

    def after(token):
        if token is None:
            return hb
        bits = lax.bitcast_convert_type(token, jnp.uint32)
        zero = jnp.right_shift(jnp.right_shift(bits, 16), 16).astype(F32).astype(BF16)
        rows, lanes = zero.shape
        first = jnp.concatenate([hb[:rows, :lanes] + zero, hb[:rows, lanes:]], axis=1)
        return jnp.concatenate([first, hb[rows:, :]], axis=0)

    def token_of(val):
        return val[:2 * F32_SUBLANES, :LANES].astype(F32)

    def head_rms(p, gain):
        ms = _dot((p * p).astype(BF16), grp_ref[...])
        return p * lax.rsqrt(ms + EPS) * gain

    def store_transposed(ref, val):
        for a in range(tm // TR_BLOCK):
            ref[0, a] = val[a * TR_BLOCK:(a + 1) * TR_BLOCK, :].T.astype(BF16)

    row = lax.broadcasted_iota(jnp.int32, (F32_SUBLANES, 1), 0)

    def conv_silu(p_and_halo, slot):
        p, ph = p_and_halo
        taps = cw[:, slot * DN_WIDTH:(slot + 1) * DN_WIDTH]
        acc = p * taps[CONV_WIDTH - 1:CONV_WIDTH, :]
        for back in range(1, CONV_WIDTH):
            shifted = pltpu.roll(p, back, axis=0)
            first = jnp.where(row < back, pltpu.roll(ph, back, axis=0), shifted[:F32_SUBLANES, :])
            src = jnp.concatenate([first, shifted[F32_SUBLANES:, :]], axis=0)
            acc = acc + src * taps[CONV_WIDTH - 1 - back:CONV_WIDTH - back, :]
        return _silu(acc)

    def l2n(y, scale):
        outs = []
        for h in range(DN_HEADS):
            yh = y[:, h * DN_HEAD_DIM:(h + 1) * DN_HEAD_DIM]
            ss = jnp.sum(yh * yh, axis=-1, keepdims=True)
            outs.append(yh * (lax.rsqrt(ss + EPS) * scale))
        return jnp.concatenate(outs, axis=1)

    def gates(pba):
        beta = _sigmoid(pba)
        ld = -jnp.exp(alog_ref[...]) * _softplus(pba + dtb_ref[...])
        lane = lax.broadcasted_iota(jnp.int32, (tm, LANES), 1)
        ld = jnp.where((lane >= DN_HEADS) & (lane < 3 * DN_HEADS), ld, 0.0)
        ri = lax.broadcasted_iota(jnp.int32, (tm, tm), 0)
        ci = lax.broadcasted_iota(jnp.int32, (tm, tm), 1)
        same = _block_of(ri, DN_CHUNK) == _block_of(ci, DN_CHUNK)
        lower = jnp.where(same & (ci <= ri), 1.0, 0.0).astype(BF16)
        wide = _dot(lower, jnp.concatenate(_split3(ld), axis=1))
        gcum = wide[:, 0:LANES] + wide[:, LANES:2 * LANES] + wide[:, 2 * LANES:]
        gtot = jnp.concatenate(
            [jnp.broadcast_to(gcum[c * DN_CHUNK + DN_CHUNK - 1:(c + 1) * DN_CHUNK, :], (DN_CHUNK, LANES))
             for c in range(tm // DN_CHUNK)], axis=0)
        g = jnp.where(lane < DN_HEADS, beta, jnp.where(lane < 2 * DN_HEADS, gcum, gtot))
        gc_ref[...] = g
        gr_ref[0] = g.T[0:GATE_ROWS, :]
        return g

    def put(ref, fn):
        def epilogue(p):
            val = fn(p)
            ref[...] = val.astype(ref.dtype)
            return token_of(val)
        return epilogue

    def project(weights):
        return lambda token: _dot(after(token), weights())

    def project_with_halo(g):
        def run(token):
            res = _dot(jnp.concatenate([hh, after(token)], axis=0), main(g))
            halo = res[HALO_ROWS - F32_SUBLANES:HALO_ROWS, :]
            return res[HALO_ROWS:, :], jnp.where(seq_start, 0.0, halo)
        return run

    def value_t(p):
        store_transposed(vt_ref, p)
        return token_of(p)

    def delta_k(p_and_halo):
        yk = l2n(conv_silu(p_and_halo, 1), 1.0)
        dk_ref[...] = yk.astype(BF16)
        store_transposed(dkt_ref, yk)
        return token_of(yk)

    stages = [
        (project_with_halo(4), put(dq_ref, lambda ph: l2n(conv_silu(ph, 0), DN_HEAD_DIM ** -0.5))),
        (project(lambda: wba_ref[...]), lambda p: token_of(gates(p))),
        (project(lambda: main(0)), put(q_ref, lambda p: head_rms(p, gq_ref[...]))),
        (project_with_halo(5), delta_k),
        (project(lambda: main(1)), put(k_ref, lambda p: head_rms(p, gk_ref[...]))),
        (project_with_halo(6), put(dv_ref, lambda ph: conv_silu(ph, 2))),
        (project(lambda: main(2)), value_t),
        (project(lambda: wmg_ref[:, 0:d_model]), put(msb_ref, _sigmoid)),
        (project(lambda: main(3)), put(z_ref, _silu)),
        (project(lambda: wmg_ref[:, d_model:2 * d_model]), put(mdn_ref, _sigmoid)),
        (project(lambda: main(7)), put(dz_ref, _silu)),
    ]
    tokens, pending = [], [stages[g][0](None) for g in range(PIPE_DEPTH)]
    for index in range(len(stages)):
        tokens.append(stages[index][1](pending[index]))
        ahead = index + PIPE_DEPTH
        if ahead < len(stages):
            pending.append(stages[ahead][0](tokens[index]))


def _inproj(x2, scale, shift, norm_g, wm, wba, wmg, gq, gk, grp, cw, alog, dtb, *, batch, seq):
    n, d = x2.shape
    tm = TOKEN_TILE
    s_tiles = seq // tm
    n_sub = tm // TR_BLOCK
    n_blk = seq // TR_BLOCK
    tile = lambda w: pl.BlockSpec((tm, w), lambda t: (t, 0))
    full = lambda a: pl.BlockSpec(a.shape, lambda t: (0,) * a.ndim, pipeline_mode=pl.Buffered(1))
    per_batch = pl.BlockSpec((1, 1, d), lambda t: (t // s_tiles, 0, 0))
    tr_spec = pl.BlockSpec((1, n_sub, SB_WIDTH, TR_BLOCK), lambda t: (t // s_tiles, t % s_tiles, 0, 0))
    tr_shape = jax.ShapeDtypeStruct((batch, n_blk, SB_WIDTH, TR_BLOCK), BF16)
    tok = lambda w, dt=BF16: jax.ShapeDtypeStruct((n, w), dt)
    out_shape = (
        tok(SB_WIDTH), tok(SB_WIDTH), tr_shape, tok(SB_WIDTH),
        tok(DN_WIDTH), tok(DN_WIDTH), tr_shape, tok(DN_WIDTH), tok(DN_WIDTH),
        tok(LANES, F32),
        jax.ShapeDtypeStruct((batch, GATE_ROWS, seq), F32),
        tok(d), tok(d),
    )
    out_specs = (
        tile(SB_WIDTH), tile(SB_WIDTH), tr_spec, tile(SB_WIDTH),
        tile(DN_WIDTH), tile(DN_WIDTH), tr_spec, tile(DN_WIDTH), tile(DN_WIDTH),
        tile(LANES),
        pl.BlockSpec((1, GATE_ROWS, tm), lambda t: (t // s_tiles, 0, t % s_tiles)),
        tile(d), tile(d),
    )
    halo_blocks = tm // HALO_ROWS
    in_specs = [
        tile(d),
        pl.BlockSpec((HALO_ROWS, d), lambda t: (jnp.maximum(t * halo_blocks - 1, 0), 0)),
        per_batch, per_batch, full(norm_g), full(wm), full(wba), full(wmg), full(gq), full(gk),
        full(grp), full(cw), full(alog), full(dtb),
    ]
    return pl.pallas_call(
        functools.partial(_inproj_kernel, tm=tm, s_tiles=s_tiles),
        out_shape=out_shape,
        grid=(n // tm,),
        in_specs=in_specs,
        out_specs=out_specs,
        compiler_params=pltpu.CompilerParams(
            dimension_semantics=("arbitrary",), vmem_limit_bytes=VMEM_LIMIT),
        name="inproj",
    )(x2, x2, scale, shift, norm_g, wm, wba, wmg, gq, gk, grp, cw, alog, dtb)


def _attn_kernel(q_ref, k_ref, vt_ref, z_ref, o_ref, *, tq, tk):
    i = pl.program_id(2)
    width = q_ref.shape[1]
    heads = width // SB_HEAD_DIM
    ks = lax.broadcasted_iota(jnp.int32, (tk, tq), 0)
    qt = lax.broadcasted_iota(jnp.int32, (tk, tq), 1)
    causal = ks < qt
    ur = lax.broadcasted_iota(jnp.int32, (tk, tk), 0)
    uc = lax.broadcasted_iota(jnp.int32, (tk, tk), 1)
    upper = jnp.where(uc > ur, 1.0, 0.0).astype(BF16)
    lane = lax.broadcasted_iota(jnp.int32, (tq, width), 1)
    q = q_ref[...]
    qhs = [jnp.where(_block_of(lane, SB_HEAD_DIM) == hh, q, jnp.zeros_like(q)) for hh in range(heads)]

    def scores(units):
        st = [_dot_nt(k_ref[0, pl.ds(pl.multiple_of(j * tk, tk), tk), :], qh) for j, qh, _ in units]
        sp = [jnp.maximum(s, 0.0) + jnp.log(1.0 + jnp.exp2(_neg_abs(s))) * LOG2E for s in st]
        spm = [p if u[2] is None else jnp.where(u[2], p, 0.0) for p, u in zip(sp, units)]
        later = [_dot(upper, m.astype(BF16)) for m in spm]
        base = [s - p - l for s, p, l in zip(st, sp, later)]
        sums = [l[0:1, :] + m[0:1, :] for l, m in zip(later, spm)]
        return base, sums

    def weights(base, carry, mask):
        w = jnp.exp2(base - carry)
        if mask is not None:
            w = jnp.where(mask, w, 0.0)
        return w.astype(BF16)

    def values(j, hh):
        return vt_ref[0, j, hh * SB_HEAD_DIM:(hh + 1) * SB_HEAD_DIM, :]

    j_hi = 2 * i + 1
    j_lo = 2 * i
    causal_hi = causal[:, :tk]

    def diagonal_units():
        units = []
        for hh in range(heads):
            units += [(j_hi, qhs[hh][tk:, :], causal_hi), (j_lo, qhs[hh], causal)]
        return units

    def diagonal_weights(base, sums):
        w_hi, w_lo, carry = [], [], []
        for hh in range(heads):
            carry_hi = jnp.concatenate([jnp.zeros((1, tk), F32), sums[2 * hh]], axis=1)
            w_hi.append(weights(base[2 * hh], 0.0, causal_hi))
            w_lo.append(weights(base[2 * hh + 1], carry_hi, causal))
            carry.append(carry_hi + sums[2 * hh + 1])
        return w_hi, w_lo, carry

    def pair_units(j):
        units = []
        for hh in range(heads):
            units += [(j, qhs[hh], None), (j - 1, qhs[hh], None)]
        return units

    def pair_weights(base, sums, carry_in):
        ws, carry_out = [], []
        for hh in range(heads):
            w_near = weights(base[2 * hh], carry_in[hh], None)
            carry = carry_in[hh] + sums[2 * hh]
            w_far = weights(base[2 * hh + 1], carry, None)
            ws.append(jnp.concatenate([w_near, w_far], axis=0))
            carry_out.append(carry + sums[2 * hh + 1])
        return ws, carry_out

    def pair_values(j, hh):
        return jnp.concatenate([values(j, hh), values(j - 1, hh)], axis=1)

    def widen(acc_hi):
        return jnp.concatenate([jnp.zeros((SB_HEAD_DIM, tk), F32), acc_hi], axis=1)

    def finish(acc):
        o = jnp.concatenate(acc, axis=0).T
        o_ref[...] = (o * z_ref[...].astype(F32)).astype(BF16)

    @pl.when(i == 0)
    def _():
        base, sums = scores(diagonal_units())
        w_hi, w_lo, _ = diagonal_weights(base, sums)
        acc_hi = [_dot(values(j_hi, hh), w_hi[hh]) for hh in range(heads)]
        acc_lo = [_dot(values(j_lo, hh), w_lo[hh]) for hh in range(heads)]
        finish([lo + widen(hi) for lo, hi in zip(acc_lo, acc_hi)])

    @pl.when(i > 0)
    def _():
        n_diag = 2 * heads
        base, sums = scores(diagonal_units() + pair_units(2 * i - 1))
        w_hi, w_lo, carry = diagonal_weights(base[:n_diag], sums[:n_diag])
        w_pair, carry = pair_weights(base[n_diag:], sums[n_diag:], carry)
        acc_hi = [_dot(values(j_hi, hh), w_hi[hh]) for hh in range(heads)]
        acc_rest = [_dot(jnp.concatenate([values(j_lo, hh), pair_values(2 * i - 1, hh)], axis=1),
                         jnp.concatenate([w_lo[hh], w_pair[hh]], axis=0)) for hh in range(heads)]
        acc = [rest + widen(hi) for rest, hi in zip(acc_rest, acc_hi)]

        def least(carry):
            return jnp.min(functools.reduce(jnp.minimum, carry))

        def cond(st):
            return (st[0] >= 1) & (st[1] < EXIT_LOG2)

        def body(st):
            j, carry, acc = st[0], list(st[2:2 + heads]), list(st[2 + heads:])
            base, sums = scores(pair_units(j))
            ws, carry = pair_weights(base, sums, carry)
            acc = [a + _dot(pair_values(j, hh), ws[hh]) for hh, a in enumerate(acc)]
            return (j - 2, least(carry), *carry, *acc)

        final = lax.while_loop(cond, body, (2 * i - 3, least(carry), *carry, *acc))
        finish(list(final[2 + heads:]))


def _attention(q, k, vt, z, *, batch, seq):
    tq, tk = ATTN_Q, TR_BLOCK
    width = ATTN_HEADS * SB_HEAD_DIM
    assert tq == 2 * tk and width % LANES == 0 and SB_WIDTH % width == 0
    n_q = seq // tq
    groups = SB_WIDTH // width
    k3 = k.reshape(batch, seq, SB_WIDTH)
    qspec = pl.BlockSpec((tq, width), lambda b, p, i: (b * n_q + i, p))
    return pl.pallas_call(
        functools.partial(_attn_kernel, tq=tq, tk=tk),
        out_shape=jax.ShapeDtypeStruct(q.shape, BF16),
        grid=(batch, groups, n_q),
        in_specs=[
            qspec,
            pl.BlockSpec((1, seq, width), lambda b, p, i: (b, 0, p)),
            pl.BlockSpec((1, seq // tk, width, tk), lambda b, p, i: (b, 0, p, 0)),
            qspec,
        ],
        out_specs=qspec,
        compiler_params=pltpu.CompilerParams(
            dimension_semantics=("arbitrary", "arbitrary", "arbitrary"), vmem_limit_bytes=VMEM_LIMIT),
        name="sb_attention",
    )(q, k3, vt, z)


def _inverse_minus_identity(ms, ri, ci, size):
    mm = lambda a, b: _dot(a.astype(BF16), b.astype(BF16))
    each = lambda f, *lists: [f(*xs) for xs in zip(*lists)]
    base = _block_of(ri, INV_BASE) == _block_of(ci, INV_BASE)
    md = each(lambda m: jnp.where(base, m, 0.0), ms)
    m2 = each(lambda a: mm(a, a), md)
    m4 = each(lambda a: mm(a, a), m2)
    m8 = each(lambda a: mm(a, a), m4)
    n = each(lambda a, b: b - a - mm(a, b), md, m2)
    n = each(lambda a, b: a + b + mm(a, b), n, m4)
    n = each(lambda a, b: a + b + mm(a, b), n, m8)
    width = INV_BASE
    while width < size:
        inner = _block_of(ri, width) == _block_of(ci, width)
        outer = _block_of(ri, 2 * width) == _block_of(ci, 2 * width)
        below = outer & jnp.logical_not(inner)
        mo = each(lambda m: jnp.where(below, m, 0.0), ms)
        x = each(lambda a, b: b + mm(a, b), n, mo)
        n = each(lambda a, b: a - b - mm(b, a), n, x)
        width *= 2
    return n


def _dn_kernel(q_ref, k_ref, kt_ref, v_ref, z_ref, gc_ref, gr_ref, ng_ref, o_ref, state_ref, *, blk):
    @pl.when(pl.program_id(1) == 0)
    def _():
        state_ref[...] = jnp.zeros_like(state_ref)

    cs = DN_CHUNK
    ri = lax.broadcasted_iota(jnp.int32, (cs, cs), 0)
    ci = lax.broadcasted_iota(jnp.int32, (cs, cs), 1)
    tril = ci <= ri
    strict = ci < ri
    gates = gc_ref[...]
    grows = gr_ref[0]
    each = lambda f, *lists: [f(*xs) for xs in zip(*lists)]
    units = [(c, h) for c in range(blk // cs) for h in range(DN_HEADS)]
    rows = lambda c: slice(c * cs, (c + 1) * cs)
    cols = lambda h: slice(h * DN_HEAD_DIM, (h + 1) * DN_HEAD_DIM)
    q = [q_ref[rows(c), cols(h)] for c, h in units]
    k = [k_ref[rows(c), cols(h)] for c, h in units]
    beta = [gates[rows(c), h:h + 1] for c, h in units]
    g = [gates[rows(c), DN_HEADS + h:DN_HEADS + h + 1] for c, h in units]
    g_row = [grows[DN_HEADS + h:DN_HEADS + h + 1, rows(c)] for c, h in units]
    gt_row = [grows[2 * DN_HEADS + h:2 * DN_HEADS + h + 1, rows(c)] for c, h in units]
    eg = each(jnp.exp, g)
    kb = each(lambda a, b: a.astype(F32) * b, k, beta)
    vb = [v_ref[rows(c), cols(h)].astype(F32) * b for (c, h), b in zip(units, beta)]
    decay = each(lambda a, b: jnp.exp(jnp.where(tril, a - b, NEG_BIG)), g, g_row)
    m = each(lambda a, b, d: jnp.where(strict, _dot_nt(a.astype(BF16), b) * d, 0.0), kb, k, decay)
    n = each(lambda a: a.astype(BF16), _inverse_minus_identity(m, ri, ci, cs))
    uw = each(lambda a, b, e: jnp.concatenate([a, b * e], axis=1), vb, kb, eg)
    uw = each(lambda a, b: b + _dot(a, b.astype(BF16)), n, uw)
    intra = each(lambda a, b, d: (_dot_nt(a, b) * d).astype(BF16), q, k, decay)
    ktd = [(kt_ref[0, c, cols(h), :].astype(F32) * jnp.exp(t - r)).astype(BF16)
           for (c, h), t, r in zip(units, gt_row, g_row)]
    carry = each(lambda t: jnp.exp(t[:, 0:1]), gt_row)
    uwb = each(lambda a: a.astype(BF16), uw)
    kuw = each(_dot, ktd, uwb)
    iuw = each(_dot, intra, uwb)
    lhs = each(lambda a, b, c, e: jnp.concatenate(
        [a[:, DN_HEAD_DIM:], c.astype(F32) * e - b[:, DN_HEAD_DIM:]], axis=0).astype(BF16),
        kuw, iuw, q, eg)

    state = [state_ref[h] for h in range(DN_HEADS)]
    outs = []
    for c in range(blk // cs):
        of = lambda xs: xs[c * DN_HEADS:(c + 1) * DN_HEADS]
        prod = each(lambda a, s: _dot(a, s.astype(BF16)), of(lhs), state)
        outs.append(each(lambda p, b: p[DN_HEAD_DIM:] + b[:, :DN_HEAD_DIM], prod, of(iuw)))
        state = each(lambda s, e, p, a: s * e - p[:DN_HEAD_DIM] + a[:, :DN_HEAD_DIM],
                     state, of(carry), prod, of(kuw))
    for h in range(DN_HEADS):
        state_ref[h] = state[h]
        o = jnp.concatenate([chunk_out[h] for chunk_out in outs], axis=0)
        ms = jnp.mean(o * o, axis=-1, keepdims=True)
        on = o * lax.rsqrt(ms + EPS) * ng_ref[...]
        o_ref[:, cols(h)] = (on * z_ref[:, cols(h)].astype(F32)).astype(BF16)


def _deltanet(dq, dk, dkt, dv, dz, gc, gr, ng, *, batch, seq):
    blk = DN_BLOCK
    n_blk = seq // blk
    tok = pl.BlockSpec((blk, DN_WIDTH), lambda b, c: (b * n_blk + c, 0))
    return pl.pallas_call(
        functools.partial(_dn_kernel, blk=blk),
        out_shape=jax.ShapeDtypeStruct(dq.shape, BF16),
        grid=(batch, n_blk),
        in_specs=[
            tok, tok,
            pl.BlockSpec((1, blk // TR_BLOCK, DN_WIDTH, TR_BLOCK), lambda b, c: (b, c, 0, 0)),
            tok, tok,
            pl.BlockSpec((blk, LANES), lambda b, c: (b * n_blk + c, 0)),
            pl.BlockSpec((1, GATE_ROWS, blk), lambda b, c: (b, 0, c)),
            pl.BlockSpec((1, DN_HEAD_DIM), lambda b, c: (0, 0)),
        ],
        out_specs=tok,
        scratch_shapes=[pltpu.VMEM((DN_HEADS, DN_HEAD_DIM, DN_HEAD_DIM), F32)],
        compiler_params=pltpu.CompilerParams(
            dimension_semantics=("arbitrary", "arbitrary"), vmem_limit_bytes=VMEM_LIMIT),
        name="deltanet",
    )(dq, dk, dkt, dv, dz, gc, gr, ng)


def _outproj_kernel(x_ref, osb_ref, odn_ref, msb_ref, mdn_ref, gate_ref, wsb_ref, wdn_ref, wo_ref, o_ref):
    ysb = _dot(osb_ref[...], wsb_ref[...])
    ydn = _dot(odn_ref[...], wdn_ref[...])
    y = msb_ref[...].astype(F32) * ysb + mdn_ref[...].astype(F32) * ydn
    out = _dot(y.astype(BF16), wo_ref[...])
    o_ref[...] = x_ref[...] + gate_ref[0] * out


def _outproj(x2, osb, odn, msb, mdn, gate, wsb, wdn, wo, *, seq):
    n, d = x2.shape
    tm = TOKEN_TILE
    s_tiles = seq // tm
    tile = lambda w: pl.BlockSpec((tm, w), lambda t: (t, 0))
    full = lambda a: pl.BlockSpec(a.shape, lambda t: (0,) * a.ndim)
    return pl.pallas_call(
        _outproj_kernel,
        out_shape=jax.ShapeDtypeStruct((n, d), F32),
        grid=(n // tm,),
        in_specs=[tile(d), tile(SB_WIDTH), tile(DN_WIDTH), tile(d), tile(d),
                  pl.BlockSpec((1, 1, d), lambda t: (t // s_tiles, 0, 0)),
                  full(wsb), full(wdn), full(wo)],
        out_specs=tile(d),
        compiler_params=pltpu.CompilerParams(
            dimension_semantics=("arbitrary",), vmem_limit_bytes=VMEM_LIMIT),
        name="outproj",
    )(x2, osb, odn, msb, mdn, gate, wsb, wdn, wo)


def _layer(x2, mod, norm_g, w_in, sb_q_g, sb_k_g, conv_w, dn_a_log, dn_dt_bias, dn_norm_g,
           w_branch_sb, w_branch_dn, w_out, *, batch, seq):
    d = x2.shape[1]
    shift = mod[:batch, 0:d].reshape(batch, 1, d)
    scale = mod[:batch, d:2 * d].reshape(batch, 1, d)
    gate = mod[:batch, 2 * d:3 * d].reshape(batch, 1, d)

    main_cols = 4 * SB_WIDTH + 4 * DN_WIDTH
    wm = w_in[:, :main_cols].astype(BF16)
    w_b = w_in[:, main_cols:main_cols + DN_HEADS]
    w_a = w_in[:, main_cols + DN_HEADS:main_cols + 2 * DN_HEADS]
    pad = jnp.zeros((d, LANES - 3 * DN_HEADS), F32)
    wba = jnp.concatenate([w_b, w_a, w_a, pad], axis=1).astype(BF16)
    wmg = w_in[:, main_cols + 2 * DN_HEADS:].astype(BF16)

    def gate_lanes(v):
        z = jnp.zeros((DN_HEADS,), F32)
        return jnp.concatenate([z, v, v, jnp.zeros((LANES - 3 * DN_HEADS,), F32)]).reshape(1, LANES)

    gq = (jnp.tile(sb_q_g, SB_HEADS) * (LOG2E * SB_HEAD_DIM ** -0.5)).reshape(1, SB_WIDTH)
    gk = jnp.tile(sb_k_g, SB_HEADS).reshape(1, SB_WIDTH)
    head_of = jnp.arange(SB_WIDTH) // SB_HEAD_DIM
    grp = jnp.where(head_of[:, None] == head_of[None, :], 1.0 / SB_HEAD_DIM, 0.0).astype(BF16)

    (q, k, vt, z, dq, dk, dkt, dv, dz, gc, gr, msb, mdn) = _inproj(
        x2, scale, shift, norm_g.reshape(1, d), wm, wba, wmg, gq, gk, grp, conv_w,
        gate_lanes(dn_a_log), gate_lanes(dn_dt_bias), batch=batch, seq=seq)
    osb = _attention(q, k, vt, z, batch=batch, seq=seq)
    odn = _deltanet(dq, dk, dkt, dv, dz, gc, gr, dn_norm_g.reshape(1, DN_HEAD_DIM), batch=batch, seq=seq)
    return _outproj(x2, osb, odn, msb, mdn, gate, w_branch_sb.astype(BF16), w_branch_dn.astype(BF16),
                    w_out.astype(BF16), seq=seq)


def kernel(x, c, ada_w, ada_b, norm_g, w_in, sb_q_g, sb_k_g, conv_w, dn_a_log, dn_dt_bias, dn_norm_g,
           w_branch_sb, w_branch_dn, w_out):
    batch, seq, d = x.shape
    depth = ada_w.shape[0]
    assert seq % TOKEN_TILE == 0 and TOKEN_TILE % DN_BLOCK == 0 and TOKEN_TILE % ATTN_Q == 0
    assert DN_CHUNK == TR_BLOCK and DN_BLOCK % DN_CHUNK == 0
    rows = 16
    c_pad = jnp.concatenate([c, jnp.zeros((rows - batch, d), c.dtype)], axis=0) if batch < rows else c
    mod = _adaln(c_pad, ada_w, ada_b)
    x2 = x.reshape(batch * seq, d)
    for l in range(depth):
        x2 = _layer(x2, mod[l], norm_g[l], w_in[l], sb_q_g[l], sb_k_g[l], conv_w[l], dn_a_log[l],
                    dn_dt_bias[l], dn_norm_g[l], w_branch_sb[l], w_branch_dn[l], w_out[l],
                    batch=batch, seq=seq)
    return x2.reshape(batch, seq, d)
```

```python
import functools

import jax
import jax.numpy as jnp
from jax import lax
from jax.experimental import pallas as pl
from jax.experimental.pallas import tpu as pltpu

F32 = jnp.float32
BF16 = jnp.bfloat16

EPS = 1e-6
SB_HEADS = 8
SB_HEAD_DIM = 64
SB_WIDTH = SB_HEADS * SB_HEAD_DIM
DN_HEADS = 4
DN_HEAD_DIM = 128
DN_WIDTH = DN_HEADS * DN_HEAD_DIM
CONV_WIDTH = 4

LANES = 128
MXU_TILE = 256
F32_SUBLANES = 8
HALO_ROWS = 16
GATE_ROWS = 16
TOKEN_TILE = 512
PIPE_DEPTH = 3
ATTN_Q = 256
ATTN_HEADS = 4
ATTN_REPS = 4
TR_BLOCK = 128
DN_BLOCK = 512
DN_CHUNK = 128
INV_BASE = 16
VMEM_LIMIT = 56 * 1024 * 1024
NEG_BIG = -1e30
LOG2E = 1.4426950408889634
EXIT_LOG2 = 152.0


def _dot(a, b):
    return jnp.dot(a, b, preferred_element_type=F32)


def _dot_nt(a, b):
    return lax.dot_general(a, b, (((1,), (1,)), ((), ())), preferred_element_type=F32)


def _sigmoid(x):
    return 0.5 * jnp.tanh(0.5 * x) + 0.5


def _silu(x):
    h = 0.5 * x
    return h + h * jnp.tanh(h)


def _neg_abs(x):
    bits = lax.bitcast_convert_type(x, jnp.uint32) | jnp.uint32(0x80000000)
    return lax.bitcast_convert_type(bits, F32)


def _softplus(x):
    return jnp.maximum(x, 0.0) + jnp.log1p(jnp.exp(-jnp.abs(x)))


def _block_of(idx, size):
    assert size & (size - 1) == 0
    return jnp.right_shift(idx, size.bit_length() - 1)


def _tied(operand, token):
    bits = lax.bitcast_convert_type(token, jnp.uint32)
    zero = jnp.right_shift(jnp.right_shift(bits, 16), 16).astype(F32)
    zero = jnp.concatenate([zero, zero], axis=0).astype(BF16)
    rows, lanes = zero.shape
    first = jnp.concatenate([operand[:rows, :lanes] + zero, operand[:rows, lanes:]], axis=1)
    return jnp.concatenate([first, operand[rows:, :]], axis=0)


def _split3(x):
    hi = x.astype(BF16)
    r = x - hi.astype(F32)
    mid = r.astype(BF16)
    lo = (r - mid.astype(F32)).astype(BF16)
    return hi, mid, lo


def _adaln_kernel(c_ref, w_ref, b_ref, o_ref):
    c = c_ref[...]
    a = c * _sigmoid(c)
    w = w_ref[0]
    a_hi = a.astype(BF16)
    a_lo = (a - a_hi.astype(F32)).astype(BF16)
    w_hi = w.astype(BF16)
    w_lo = (w - w_hi.astype(F32)).astype(BF16)
    o_ref[0] = _dot(a_hi, w_hi) + _dot(a_hi, w_lo) + _dot(a_lo, w_hi) + b_ref[0]


def _adaln(c_pad, ada_w, ada_b):
    depth, d, d3 = ada_w.shape
    rows = c_pad.shape[0]
    tn = 512
    return pl.pallas_call(
        _adaln_kernel,
        out_shape=jax.ShapeDtypeStruct((depth, rows, d3), F32),
        grid=(depth, d3 // tn),
        in_specs=[
            pl.BlockSpec((rows, d), lambda l, j: (0, 0)),
            pl.BlockSpec((1, d, tn), lambda l, j: (l, 0, j)),
            pl.BlockSpec((1, 1, tn), lambda l, j: (l, 0, j)),
        ],
        out_specs=pl.BlockSpec((1, rows, tn), lambda l, j: (l, 0, j)),
        compiler_params=pltpu.CompilerParams(
            dimension_semantics=("arbitrary", "arbitrary"), vmem_limit_bytes=VMEM_LIMIT),
        name="adaln",
    )(c_pad, ada_w, ada_b.reshape(depth, 1, d3))


def _inproj_kernel(x_ref, xh_ref, sc_ref, sh_ref, ng_ref, wm_ref, wba_ref, wmg_ref, gq_ref, gk_ref,
                   grp_ref, cw_ref, alog_ref, dtb_ref,
                   q_ref, k_ref, vt_ref, z_ref, dq_ref, dk_ref, dkt_ref, dv_ref, dz_ref,
                   gc_ref, gr_ref, msb_ref, mdn_ref, *, tm, s_tiles):
    t = pl.program_id(0)
    gmod = ng_ref[...] * (1.0 + sc_ref[0])
    shift = sh_ref[0]

    def norm_mod(xv):
        ms = jnp.mean(xv * xv, axis=-1, keepdims=True)
        return (xv * lax.rsqrt(ms + EPS) * gmod + shift).astype(BF16)

    hb = norm_mod(x_ref[...])
    hh = norm_mod(xh_ref[...])
    seq_start = (t % s_tiles) == 0
    cw = cw_ref[...]
    d_model = msb_ref.shape[1]

    def main(g):
        return wm_ref[:, g * SB_WIDTH:(g + 1) * SB_WIDTH]

    def after(token):
        return hb if token is None else _tied(hb, token)

    def token_of(val):
        return val[:F32_SUBLANES, :LANES].astype(F32)

    def head_rms(p, gain):
        sq = (p * p).astype(BF16)
        ms = jnp.concatenate([_dot(sq[:, a:a + MXU_TILE], grp_ref[...])
                              for a in range(0, SB_WIDTH, MXU_TILE)], axis=1)
        return p * lax.rsqrt(ms + EPS) * gain

    def store_transposed(ref, val):
        for a in range(tm // TR_BLOCK):
            ref[0, a] = val[a * TR_BLOCK:(a + 1) * TR_BLOCK, :].T.astype(BF16)

    row = lax.broadcasted_iota(jnp.int32, (F32_SUBLANES, 1), 0)

    def conv_silu(p_and_halo, slot):
        p, ph = p_and_halo
        taps = cw[:, slot * DN_WIDTH:(slot + 1) * DN_WIDTH]
        acc = p * taps[CONV_WIDTH - 1:CONV_WIDTH, :]
        for back in range(1, CONV_WIDTH):
            shifted = pltpu.roll(p, back, axis=0)
            first = jnp.where(row < back, pltpu.roll(ph, back, axis=0), shifted[:F32_SUBLANES, :])
            src = jnp.concatenate([first, shifted[F32_SUBLANES:, :]], axis=0)
            acc = acc + src * taps[CONV_WIDTH - 1 - back:CONV_WIDTH - back, :]
        return _silu(acc)

    def l2n(y, scale):
        outs = []
        for h in range(DN_HEADS):
            yh = y[:, h * DN_HEAD_DIM:(h + 1) * DN_HEAD_DIM]
            ss = jnp.sum(yh * yh, axis=-1, keepdims=True)
            outs.append(yh * (lax.rsqrt(ss + EPS) * scale))
        return jnp.concatenate(outs, axis=1)

    def gates(pba):
        beta = _sigmoid(pba)
        ld = -jnp.exp(alog_ref[...]) * _softplus(pba + dtb_ref[...])
        lane = lax.broadcasted_iota(jnp.int32, (tm, LANES), 1)
        ld = jnp.where((lane >= DN_HEADS) & (lane < 3 * DN_HEADS), ld, 0.0)
        ri = lax.broadcasted_iota(jnp.int32, (tm, tm), 0)
        ci = lax.broadcasted_iota(jnp.int32, (tm, tm), 1)
        same = _block_of(ri, DN_CHUNK) == _block_of(ci, DN_CHUNK)
        lower = jnp.where(same & (ci <= ri), 1.0, 0.0).astype(BF16)
        wide = _dot(lower, jnp.concatenate(_split3(ld), axis=1))
        gcum = wide[:, 0:LANES] + wide[:, LANES:2 * LANES] + wide[:, 2 * LANES:]
        gtot = jnp.concatenate(
            [jnp.broadcast_to(gcum[c * DN_CHUNK + DN_CHUNK - 1:(c + 1) * DN_CHUNK, :], (DN_CHUNK, LANES))
             for c in range(tm // DN_CHUNK)], axis=0)
        g = jnp.where(lane < DN_HEADS, beta, jnp.where(lane < 2 * DN_HEADS, gcum, gtot))
        gc_ref[...] = g
        gr_ref[0] = g.T[0:GATE_ROWS, :]
        return g

    def put(ref, fn):
        def epilogue(p):
            val = fn(p)
            ref[...] = val.astype(ref.dtype)
            return token_of(val)
        return epilogue

    def project(weights):
        return lambda token: _dot(after(token), weights())

    def project_with_halo(g):
        def run(token):
            res = _dot(jnp.concatenate([hh, after(token)], axis=0), main(g))
            halo = res[HALO_ROWS - F32_SUBLANES:HALO_ROWS, :]
            return res[HALO_ROWS:, :], jnp.where(seq_start, 0.0, halo)
        return run

    def value_t(p):
        store_transposed(vt_ref, p)
        return token_of(p)

    def delta_k(p_and_halo):
        yk = l2n(conv_silu(p_and_halo, 1), 1.0)
        dk_ref[...] = yk.astype(BF16)
        store_transposed(dkt_ref, yk)
        return token_of(yk)

    stages = [
        (project_with_halo(4), put(dq_ref, lambda ph: l2n(conv_silu(ph, 0), DN_HEAD_DIM ** -0.5))),
        (project(lambda: wba_ref[...]), lambda p: token_of(gates(p))),
        (project(lambda: main(0)), put(q_ref, lambda p: head_rms(p, gq_ref[...]))),
        (project_with_halo(5), delta_k),
        (project(lambda: main(1)), put(k_ref, lambda p: head_rms(p, gk_ref[...]))),
        (project_with_halo(6), put(dv_ref, lambda ph: conv_silu(ph, 2))),
        (project(lambda: main(2)), value_t),
        (project(lambda: wmg_ref[:, 0:d_model]), put(msb_ref, _sigmoid)),
        (project(lambda: main(3)), put(z_ref, _silu)),
        (project(lambda: wmg_ref[:, d_model:2 * d_model]), put(mdn_ref, _sigmoid)),
        (project(lambda: main(7)), put(dz_ref, _silu)),
    ]
    tokens, pending = [], [stages[g][0](None) for g in range(PIPE_DEPTH)]
    for index in range(len(stages)):
        tokens.append(stages[index][1](pending[index]))
        ahead = index + PIPE_DEPTH
        if ahead < len(stages):
            pending.append(stages[ahead][0](tokens[index]))


def _inproj(x2, scale, shift, norm_g, wm, wba, wmg, gq, gk, grp, cw, alog, dtb, *, batch, seq):
    n, d = x2.shape
    tm = TOKEN_TILE
    s_tiles = seq // tm
    n_sub = tm // TR_BLOCK
    n_blk = seq // TR_BLOCK
    tile = lambda w: pl.BlockSpec((tm, w), lambda t: (t, 0))
    full = lambda a: pl.BlockSpec(a.shape, lambda t: (0,) * a.ndim, pipeline_mode=pl.Buffered(1))
    per_batch = pl.BlockSpec((1, 1, d), lambda t: (t // s_tiles, 0, 0))
    tr_spec = pl.BlockSpec((1, n_sub, SB_WIDTH, TR_BLOCK), lambda t: (t // s_tiles, t % s_tiles, 0, 0))
    tr_shape = jax.ShapeDtypeStruct((batch, n_blk, SB_WIDTH, TR_BLOCK), BF16)
    tok = lambda w, dt=BF16: jax.ShapeDtypeStruct((n, w), dt)
    out_shape = (
        tok(SB_WIDTH), tok(SB_WIDTH), tr_shape, tok(SB_WIDTH),
        tok(DN_WIDTH), tok(DN_WIDTH), tr_shape, tok(DN_WIDTH), tok(DN_WIDTH),
        tok(LANES, F32),
        jax.ShapeDtypeStruct((batch, GATE_ROWS, seq), F32),
        tok(d), tok(d),
    )
    out_specs = (
        tile(SB_WIDTH), tile(SB_WIDTH), tr_spec, tile(SB_WIDTH),
        tile(DN_WIDTH), tile(DN_WIDTH), tr_spec, tile(DN_WIDTH), tile(DN_WIDTH),
        tile(LANES),
        pl.BlockSpec((1, GATE_ROWS, tm), lambda t: (t // s_tiles, 0, t % s_tiles)),
        tile(d), tile(d),
    )
    halo_blocks = tm // HALO_ROWS
    in_specs = [
        tile(d),
        pl.BlockSpec((HALO_ROWS, d), lambda t: (jnp.maximum(t * halo_blocks - 1, 0), 0)),
        per_batch, per_batch, full(norm_g), full(wm), full(wba), full(wmg), full(gq), full(gk),
        full(grp), full(cw), full(alog), full(dtb),
    ]
    return pl.pallas_call(
        functools.partial(_inproj_kernel, tm=tm, s_tiles=s_tiles),
        out_shape=out_shape,
        grid=(n // tm,),
        in_specs=in_specs,
        out_specs=out_specs,
        compiler_params=pltpu.CompilerParams(
            dimension_semantics=("arbitrary",), vmem_limit_bytes=VMEM_LIMIT),
        name="inproj",
    )(x2, x2, scale, shift, norm_g, wm, wba, wmg, gq, gk, grp, cw, alog, dtb)


def _attn_kernel(q_ref, k_ref, vt_ref, z_ref, o_ref, *, tq, tk, reps):
    def one(r, carry):
        rows = pl.ds(pl.multiple_of(r * tq, tq), tq)
        _attn_block(pl.program_id(2) * reps + r, q_ref.at[rows], k_ref, vt_ref, z_ref.at[rows],
                    o_ref.at[rows], tq=tq, tk=tk)
        return carry

    lax.fori_loop(0, reps, one, 0)


def _attn_block(i, q_ref, k_ref, vt_ref, z_ref, o_ref, *, tq, tk):
    width = q_ref.shape[1]
    heads = width // SB_HEAD_DIM
    ks = lax.broadcasted_iota(jnp.int32, (tk, tq), 0)
    qt = lax.broadcasted_iota(jnp.int32, (tk, tq), 1)
    causal = ks < qt
    ur = lax.broadcasted_iota(jnp.int32, (tk, tk), 0)
    uc = lax.broadcasted_iota(jnp.int32, (tk, tk), 1)
    upper = jnp.where(uc > ur, 1.0, 0.0).astype(BF16)
    lane = lax.broadcasted_iota(jnp.int32, (tq, width), 1)
    q = q_ref[...]
    qhs = [jnp.where(_block_of(lane, SB_HEAD_DIM) == hh, q, jnp.zeros_like(q)) for hh in range(heads)]

    def scores(units):
        st = [_dot_nt(k_ref[0, pl.ds(pl.multiple_of(j * tk, tk), tk), :], qh) for j, qh, _ in units]
        sp = [jnp.maximum(s, 0.0) + jnp.log(1.0 + jnp.exp2(_neg_abs(s))) * LOG2E for s in st]
        spm = [p if u[2] is None else jnp.where(u[2], p, 0.0) for p, u in zip(sp, units)]
        later = [_dot(upper, m.astype(BF16)) for m in spm]
        base = [s - p - l for s, p, l in zip(st, sp, later)]
        sums = [l[0:1, :] + m[0:1, :] for l, m in zip(later, spm)]
        return base, sums

    def weights(base, carry, mask):
        w = jnp.exp2(base - carry)
        if mask is not None:
            w = jnp.where(mask, w, 0.0)
        return w.astype(BF16)

    def values(j, hh):
        return vt_ref[0, j, hh * SB_HEAD_DIM:(hh + 1) * SB_HEAD_DIM, :]

    j_hi = 2 * i + 1
    j_lo = 2 * i
    causal_hi = causal[:, :tk]

    def diagonal_units():
        units = []
        for hh in range(heads):
            units += [(j_hi, qhs[hh][tk:, :], causal_hi), (j_lo, qhs[hh], causal)]
        return units

    def diagonal_weights(base, sums):
        w_hi, w_lo, carry = [], [], []
        for hh in range(heads):
            carry_hi = jnp.concatenate([jnp.zeros((1, tk), F32), sums[2 * hh]], axis=1)
            w_hi.append(weights(base[2 * hh], 0.0, causal_hi))
            w_lo.append(weights(base[2 * hh + 1], carry_hi, causal))
            carry.append(carry_hi + sums[2 * hh + 1])
        return w_hi, w_lo, carry

    def pair_units(j):
        units = []
        for hh in range(heads):
            units += [(j, qhs[hh], None), (j - 1, qhs[hh], None)]
        return units

    def pair_weights(base, sums, carry_in):
        ws, carry_out = [], []
        for hh in range(heads):
            w_near = weights(base[2 * hh], carry_in[hh], None)
            carry = carry_in[hh] + sums[2 * hh]
            w_far = weights(base[2 * hh + 1], carry, None)
            ws.append(jnp.concatenate([w_near, w_far], axis=0))
            carry_out.append(carry + sums[2 * hh + 1])
        return ws, carry_out

    def pair_values(j, hh):
        return jnp.concatenate([values(j, hh), values(j - 1, hh)], axis=1)

    def widen(acc_hi):
        return jnp.concatenate([jnp.zeros((SB_HEAD_DIM, tk), F32), acc_hi], axis=1)

    def finish(acc):
        o = jnp.concatenate(acc, axis=0).T
        o_ref[...] = (o * z_ref[...].astype(F32)).astype(BF16)

    @pl.when(i == 0)
    def _():
        base, sums = scores(diagonal_units())
        w_hi, w_lo, _ = diagonal_weights(base, sums)
        acc_hi = [_dot(values(j_hi, hh), w_hi[hh]) for hh in range(heads)]
        acc_lo = [_dot(values(j_lo, hh), w_lo[hh]) for hh in range(heads)]
        finish([lo + widen(hi) for lo, hi in zip(acc_lo, acc_hi)])

    @pl.when(i > 0)
    def _():
        n_diag = 2 * heads
        j_c, j_d = 2 * i - 1, 2 * i - 2
        near = [(j_c, qhs[hh], None) for hh in range(heads)]
        far_lower = [(j_d, qhs[hh][:tk, :], None) for hh in range(heads)]
        base, sums = scores(diagonal_units() + near + far_lower)
        w_hi, w_lo, carry = diagonal_weights(base[:n_diag], sums[:n_diag])
        base_c, sums_c = base[n_diag:n_diag + heads], sums[n_diag:n_diag + heads]
        base_d, sums_d = base[n_diag + heads:], sums[n_diag + heads:]
        w_c = [weights(b, c, None) for b, c in zip(base_c, carry)]
        carry = [c + s for c, s in zip(carry, sums_c)]
        w_d = [weights(b, c[:, :tk], None) for b, c in zip(base_d, carry)]
        carry_lower = [c[:, :tk] + s for c, s in zip(carry, sums_d)]
        carry_upper = [c[:, tk:] for c in carry]
        acc_hi = [_dot(values(j_hi, hh), w_hi[hh]) for hh in range(heads)]
        acc_lower = [_dot(values(j_d, hh), w_d[hh]) for hh in range(heads)]
        acc_rest = [_dot(jnp.concatenate([values(j_lo, hh), values(j_c, hh)], axis=1),
                         jnp.concatenate([w_lo[hh], w_c[hh]], axis=0)) for hh in range(heads)]
        acc = [rest + jnp.concatenate([lower, hi], axis=1)
               for rest, lower, hi in zip(acc_rest, acc_lower, acc_hi)]

        def least(carry):
            return jnp.min(functools.reduce(jnp.minimum, carry))

        def upper_far(carry_upper, acc, _):
            base, sums = scores([(j_d, qhs[hh][tk:, :], None) for hh in range(heads)])
            ws = [weights(b, c, None) for b, c in zip(base, carry_upper)]
            extra = [_dot(values(j_d, hh), ws[hh]) for hh in range(heads)]
            carry_upper = [c + s for c, s in zip(carry_upper, sums)]
            return carry_upper, [a + widen(e) for a, e in zip(acc, extra)], least(carry_upper)

        least_lower, least_upper = least(carry_lower), least(carry_upper)
        carry_upper, acc, least_upper = lax.cond(least_upper < EXIT_LOG2, upper_far,
                                                 lambda c, a, m: (c, a, m), carry_upper, acc, least_upper)
        carry = [jnp.concatenate([lo, up], axis=1) for lo, up in zip(carry_lower, carry_upper)]

        def cond(st):
            return (st[0] >= 1) & (st[1] < EXIT_LOG2)

        def body(st):
            j, carry, acc = st[0], list(st[2:2 + heads]), list(st[2 + heads:])
            base, sums = scores(pair_units(j))
            ws, carry = pair_weights(base, sums, carry)
            acc = [a + _dot(pair_values(j, hh), ws[hh]) for hh, a in enumerate(acc)]
            return (j - 2, least(carry), *carry, *acc)

        final = lax.while_loop(cond, body, (2 * i - 3, jnp.minimum(least_lower, least_upper), *carry, *acc))
        finish(list(final[2 + heads:]))


def _attention(q, k, vt, z, *, batch, seq):
    tq, tk = ATTN_Q, TR_BLOCK
    width = ATTN_HEADS * SB_HEAD_DIM
    assert tq == 2 * tk and width % LANES == 0 and SB_WIDTH % width == 0
    reps = ATTN_REPS
    assert seq % (tq * reps) == 0
    n_steps = seq // (tq * reps)
    groups = SB_WIDTH // width
    k3 = k.reshape(batch, seq, SB_WIDTH)
    qspec = pl.BlockSpec((tq * reps, width), lambda b, p, i: (b * n_steps + i, p))
    return pl.pallas_call(
        functools.partial(_attn_kernel, tq=tq, tk=tk, reps=reps),
        out_shape=jax.ShapeDtypeStruct(q.shape, BF16),
        grid=(batch, groups, n_steps),
        in_specs=[
            qspec,
            pl.BlockSpec((1, seq, width), lambda b, p, i: (b, 0, p)),
            pl.BlockSpec((1, seq // tk, width, tk), lambda b, p, i: (b, 0, p, 0)),
            qspec,
        ],
        out_specs=qspec,
        compiler_params=pltpu.CompilerParams(
            dimension_semantics=("arbitrary", "arbitrary", "arbitrary"), vmem_limit_bytes=VMEM_LIMIT),
        name="sb_attention",
    )(q, k3, vt, z)


def _inverse_minus_identity(ms, ri, ci, size):
    mm = lambda a, b: _dot(a.astype(BF16), b.astype(BF16))
    each = lambda f, *lists: [f(*xs) for xs in zip(*lists)]
    base = _block_of(ri, INV_BASE) == _block_of(ci, INV_BASE)
    md = each(lambda m: jnp.where(base, m, 0.0), ms)
    m2 = each(lambda a: mm(a, a), md)
    m4 = each(lambda a: mm(a, a), m2)
    m8 = each(lambda a: mm(a, a), m4)
    n = each(lambda a, b: b - a - mm(a, b), md, m2)
    n = each(lambda a, b: a + b + mm(a, b), n, m4)
    n = each(lambda a, b: a + b + mm(a, b), n, m8)
    width = INV_BASE
    while width < size:
        inner = _block_of(ri, width) == _block_of(ci, width)
        outer = _block_of(ri, 2 * width) == _block_of(ci, 2 * width)
        below = outer & jnp.logical_not(inner)
        mo = each(lambda m: jnp.where(below, m, 0.0), ms)
        x = each(lambda a, b: b + mm(a, b), n, mo)
        n = each(lambda a, b: a - b - mm(b, a), n, x)
        width *= 2
    return n


def _dn_kernel(q_ref, k_ref, kt_ref, v_ref, z_ref, gc_ref, gr_ref, ng_ref, o_ref, state_ref, *, blk):
    @pl.when(pl.program_id(1) == 0)
    def _():
        state_ref[...] = jnp.zeros_like(state_ref)

    cs = DN_CHUNK
    ri = lax.broadcasted_iota(jnp.int32, (cs, cs), 0)
    ci = lax.broadcasted_iota(jnp.int32, (cs, cs), 1)
    tril = ci <= ri
    strict = ci < ri
    gates = gc_ref[...]
    grows = gr_ref[0]
    each = lambda f, *lists: [f(*xs) for xs in zip(*lists)]
    units = [(c, h) for c in range(blk // cs) for h in range(DN_HEADS)]
    rows = lambda c: slice(c * cs, (c + 1) * cs)
    cols = lambda h: slice(h * DN_HEAD_DIM, (h + 1) * DN_HEAD_DIM)
    q = [q_ref[rows(c), cols(h)] for c, h in units]
    k = [k_ref[rows(c), cols(h)] for c, h in units]
    beta = [gates[rows(c), h:h + 1] for c, h in units]
    g = [gates[rows(c), DN_HEADS + h:DN_HEADS + h + 1] for c, h in units]
    g_row = [grows[DN_HEADS + h:DN_HEADS + h + 1, rows(c)] for c, h in units]
    gt_row = [grows[2 * DN_HEADS + h:2 * DN_HEADS + h + 1, rows(c)] for c, h in units]
    eg = each(jnp.exp, g)
    kb = each(lambda a, b: a.astype(F32) * b, k, beta)
    vb = [v_ref[rows(c), cols(h)].astype(F32) * b for (c, h), b in zip(units, beta)]
    decay = each(lambda a, b: jnp.exp(jnp.where(tril, a - b, NEG_BIG)), g, g_row)
    m = each(lambda a, b, d: jnp.where(strict, _dot_nt(a.astype(BF16), b) * d, 0.0), kb, k, decay)
    n = each(lambda a: a.astype(BF16), _inverse_minus_identity(m, ri, ci, cs))
    uw = each(lambda a, b, e: jnp.concatenate([a, b * e], axis=1), vb, kb, eg)
    uw = each(lambda a, b: b + _dot(a, b.astype(BF16)), n, uw)
    intra = each(lambda a, b, d: (_dot_nt(a, b) * d).astype(BF16), q, k, decay)
    ktd = [(kt_ref[0, c, cols(h), :].astype(F32) * jnp.exp(t - r)).astype(BF16)
           for (c, h), t, r in zip(units, gt_row, g_row)]
    carry = each(lambda t: jnp.exp(t[:, 0:1]), gt_row)
    uwb = each(lambda a: a.astype(BF16), uw)
    kuw = each(_dot, ktd, uwb)
    iuw = each(_dot, intra, uwb)
    lhs = each(lambda a, b, c, e: jnp.concatenate(
        [a[:, DN_HEAD_DIM:], c.astype(F32) * e - b[:, DN_HEAD_DIM:]], axis=0).astype(BF16),
        kuw, iuw, q, eg)

    state = [state_ref[h] for h in range(DN_HEADS)]
    outs = []
    for c in range(blk // cs):
        of = lambda xs: xs[c * DN_HEADS:(c + 1) * DN_HEADS]
        prod = each(lambda a, s: _dot(a, s.astype(BF16)), of(lhs), state)
        outs.append(each(lambda p, b: p[DN_HEAD_DIM:] + b[:, :DN_HEAD_DIM], prod, of(iuw)))
        state = each(lambda s, e, p, a: s * e - p[:DN_HEAD_DIM] + a[:, :DN_HEAD_DIM],
                     state, of(carry), prod, of(kuw))
    for h in range(DN_HEADS):
        state_ref[h] = state[h]
        o = jnp.concatenate([chunk_out[h] for chunk_out in outs], axis=0)
        ms = jnp.mean(o * o, axis=-1, keepdims=True)
        on = o * lax.rsqrt(ms + EPS) * ng_ref[...]
        o_ref[:, cols(h)] = (on * z_ref[:, cols(h)].astype(F32)).astype(BF16)


def _deltanet(dq, dk, dkt, dv, dz, gc, gr, ng, *, batch, seq):
    blk = DN_BLOCK
    n_blk = seq // blk
    tok = pl.BlockSpec((blk, DN_WIDTH), lambda b, c: (b * n_blk + c, 0))
    return pl.pallas_call(
        functools.partial(_dn_kernel, blk=blk),
        out_shape=jax.ShapeDtypeStruct(dq.shape, BF16),
        grid=(batch, n_blk),
        in_specs=[
            tok, tok,
            pl.BlockSpec((1, blk // TR_BLOCK, DN_WIDTH, TR_BLOCK), lambda b, c: (b, c, 0, 0)),
            tok, tok,
            pl.BlockSpec((blk, LANES), lambda b, c: (b * n_blk + c, 0)),
            pl.BlockSpec((1, GATE_ROWS, blk), lambda b, c: (b, 0, c)),
            pl.BlockSpec((1, DN_HEAD_DIM), lambda b, c: (0, 0)),
        ],
        out_specs=tok,
        scratch_shapes=[pltpu.VMEM((DN_HEADS, DN_HEAD_DIM, DN_HEAD_DIM), F32)],
        compiler_params=pltpu.CompilerParams(
            dimension_semantics=("arbitrary", "arbitrary"), vmem_limit_bytes=VMEM_LIMIT),
        name="deltanet",
    )(dq, dk, dkt, dv, dz, gc, gr, ng)


def _outproj_kernel(x_ref, osb_ref, odn_ref, msb_ref, mdn_ref, gate_ref, wsb_ref, wdn_ref, wo_ref, o_ref):
    ysb = _dot(osb_ref[...], wsb_ref[...])
    ydn = _dot(odn_ref[...], wdn_ref[...])
    y = msb_ref[...].astype(F32) * ysb + mdn_ref[...].astype(F32) * ydn
    out = _dot(y.astype(BF16), wo_ref[...])
    o_ref[...] = x_ref[...] + gate_ref[0] * out


def _outproj(x2, osb, odn, msb, mdn, gate, wsb, wdn, wo, *, seq):
    n, d = x2.shape
    tm = TOKEN_TILE
    s_tiles = seq // tm
    tile = lambda w: pl.BlockSpec((tm, w), lambda t: (t, 0))
    full = lambda a: pl.BlockSpec(a.shape, lambda t: (0,) * a.ndim)
    return pl.pallas_call(
        _outproj_kernel,
        out_shape=jax.ShapeDtypeStruct((n, d), F32),
        grid=(n // tm,),
        in_specs=[tile(d), tile(SB_WIDTH), tile(DN_WIDTH), tile(d), tile(d),
                  pl.BlockSpec((1, 1, d), lambda t: (t // s_tiles, 0, 0)),
                  full(wsb), full(wdn), full(wo)],
        out_specs=tile(d),
        compiler_params=pltpu.CompilerParams(
            dimension_semantics=("arbitrary",), vmem_limit_bytes=VMEM_LIMIT),
        name="outproj",
    )(x2, osb, odn, msb, mdn, gate, wsb, wdn, wo)


def _layer(x2, mod, norm_g, w_in, sb_q_g, sb_k_g, conv_w, dn_a_log, dn_dt_bias, dn_norm_g,
           w_branch_sb, w_branch_dn, w_out, *, batch, seq):
    d = x2.shape[1]
    shift = mod[:batch, 0:d].reshape(batch, 1, d)
    scale = mod[:batch, d:2 * d].reshape(batch, 1, d)
    gate = mod[:batch, 2 * d:3 * d].reshape(batch, 1, d)

    main_cols = 4 * SB_WIDTH + 4 * DN_WIDTH
    wm = w_in[:, :main_cols].astype(BF16)
    w_b = w_in[:, main_cols:main_cols + DN_HEADS]
    w_a = w_in[:, main_cols + DN_HEADS:main_cols + 2 * DN_HEADS]
    pad = jnp.zeros((d, LANES - 3 * DN_HEADS), F32)
    wba = jnp.concatenate([w_b, w_a, w_a, pad], axis=1).astype(BF16)
    wmg = w_in[:, main_cols + 2 * DN_HEADS:].astype(BF16)

    def gate_lanes(v):
        z = jnp.zeros((DN_HEADS,), F32)
        return jnp.concatenate([z, v, v, jnp.zeros((LANES - 3 * DN_HEADS,), F32)]).reshape(1, LANES)

    gq = (jnp.tile(sb_q_g, SB_HEADS) * (LOG2E * SB_HEAD_DIM ** -0.5)).reshape(1, SB_WIDTH)
    gk = jnp.tile(sb_k_g, SB_HEADS).reshape(1, SB_WIDTH)
    head_of = jnp.arange(MXU_TILE) // SB_HEAD_DIM
    grp = jnp.where(head_of[:, None] == head_of[None, :], 1.0 / SB_HEAD_DIM, 0.0).astype(BF16)

    (q, k, vt, z, dq, dk, dkt, dv, dz, gc, gr, msb, mdn) = _inproj(
        x2, scale, shift, norm_g.reshape(1, d), wm, wba, wmg, gq, gk, grp, conv_w,
        gate_lanes(dn_a_log), gate_lanes(dn_dt_bias), batch=batch, seq=seq)
    osb = _attention(q, k, vt, z, batch=batch, seq=seq)
    odn = _deltanet(dq, dk, dkt, dv, dz, gc, gr, dn_norm_g.reshape(1, DN_HEAD_DIM), batch=batch, seq=seq)
    return _outproj(x2, osb, odn, msb, mdn, gate, w_branch_sb.astype(BF16), w_branch_dn.astype(BF16),
                    w_out.astype(BF16), seq=seq)


def kernel(x, c, ada_w, ada_b, norm_g, w_in, sb_q_g, sb_k_g, conv_w, dn_a_log, dn_dt_bias, dn_norm_g,
           w_branch_sb, w_branch_dn, w_out):
    batch, seq, d = x.shape
    depth = ada_w.shape[0]
    assert seq % TOKEN_TILE == 0 and seq % DN_BLOCK == 0 and seq % ATTN_Q == 0
    assert DN_CHUNK == TR_BLOCK and DN_BLOCK % DN_CHUNK == 0 and TOKEN_TILE % DN_CHUNK == 0
    rows = 16
    c_pad = jnp.concatenate([c, jnp.zeros((rows - batch, d), c.dtype)], axis=0) if batch < rows else c
    mod = _adaln(c_pad, ada_w, ada_b)
    x2 = x.reshape(batch * seq, d)
    for l in range(depth):
        x2 = _layer(x2, mod[l], norm_g[l], w_in[l], sb_q_g[l], sb_k_g[l], conv_w[l], dn_a_log[l],
                    dn_dt_bias[l], dn_norm_g[l], w_branch_sb[l], w_branch_dn[l], w_out[l],
                    batch=batch, seq=seq)
    return x2.reshape(batch, seq, d)
```

```python
import functools

import jax
import jax.numpy as jnp
from jax import lax
from jax.experimental import pallas as pl
from jax.experimental.pallas import tpu as pltpu

F32 = jnp.float32
BF16 = jnp.bfloat16

EPS = 1e-6
SB_HEADS = 8
SB_HEAD_DIM = 64
SB_WIDTH = SB_HEADS * SB_HEAD_DIM
DN_HEADS = 4
DN_HEAD_DIM = 128
DN_WIDTH = DN_HEADS * DN_HEAD_DIM
CONV_WIDTH = 4

LANES = 128
MXU_TILE = 256
F32_SUBLANES = 8
HALO_ROWS = 16
GATE_ROWS = 16
TOKEN_TILE = 512
OUT_TILE = 1024
PIPE_DEPTH = 3
ATTN_Q = 256
ATTN_HEADS = 4
ATTN_REPS = 16
TR_BLOCK = 128
DN_BLOCK = 512
DN_CHUNK = 128
INV_BASE = 16
V7X_VMEM_BYTES = 64 * 1024 * 1024
VMEM_LIMIT = V7X_VMEM_BYTES * 7 // 8
ADALN_ROWS = 16
NEG_BIG = -1e30
LOG2E = 1.4426950408889634
EXIT_LOG2 = 152.0


def _dot(a, b):
    return jnp.dot(a, b, preferred_element_type=F32)


def _dot_nt(a, b):
    return lax.dot_general(a, b, (((1,), (1,)), ((), ())), preferred_element_type=F32)


def _sigmoid(x):
    return 0.5 * jnp.tanh(0.5 * x) + 0.5


def _silu(x):
    h = 0.5 * x
    return h + h * jnp.tanh(h)


def _neg_abs(x):
    bits = lax.bitcast_convert_type(x, jnp.uint32) | jnp.uint32(0x80000000)
    return lax.bitcast_convert_type(bits, F32)


def _softplus(x):
    return jnp.maximum(x, 0.0) + jnp.log1p(jnp.exp(-jnp.abs(x)))


def _block_of(idx, size):
    assert size & (size - 1) == 0
    return jnp.right_shift(idx, size.bit_length() - 1)


def _tied(operand, token):
    bits = lax.bitcast_convert_type(token, jnp.uint32)
    zero = jnp.right_shift(jnp.right_shift(bits, 16), 16).astype(F32)
    zero = jnp.concatenate([zero, zero], axis=0).astype(BF16)
    rows, lanes = zero.shape
    first = jnp.concatenate([operand[:rows, :lanes] + zero, operand[:rows, lanes:]], axis=1)
    return jnp.concatenate([first, operand[rows:, :]], axis=0)


def _split3(x):
    hi = x.astype(BF16)
    r = x - hi.astype(F32)
    mid = r.astype(BF16)
    lo = (r - mid.astype(F32)).astype(BF16)
    return hi, mid, lo


def _adaln_kernel(c_ref, w_ref, b_ref, o_ref):
    c = c_ref[...]
    a = c * _sigmoid(c)
    w = w_ref[0]
    a_hi = a.astype(BF16)
    a_lo = (a - a_hi.astype(F32)).astype(BF16)
    w_hi = w.astype(BF16)
    w_lo = (w - w_hi.astype(F32)).astype(BF16)
    o_ref[0] = _dot(a_hi, w_hi) + _dot(a_hi, w_lo) + _dot(a_lo, w_hi) + b_ref[0]


def _adaln(c_pad, ada_w, ada_b):
    depth, d, d3 = ada_w.shape
    rows = c_pad.shape[0]
    tn = 512
    return pl.pallas_call(
        _adaln_kernel,
        out_shape=jax.ShapeDtypeStruct((depth, rows, d3), F32),
        grid=(depth, d3 // tn),
        in_specs=[
            pl.BlockSpec((rows, d), lambda l, j: (0, 0)),
            pl.BlockSpec((1, d, tn), lambda l, j: (l, 0, j)),
            pl.BlockSpec((1, 1, tn), lambda l, j: (l, 0, j)),
        ],
        out_specs=pl.BlockSpec((1, rows, tn), lambda l, j: (l, 0, j)),
        compiler_params=pltpu.CompilerParams(
            dimension_semantics=("arbitrary", "arbitrary"), vmem_limit_bytes=VMEM_LIMIT),
        name="adaln",
    )(c_pad, ada_w, ada_b.reshape(depth, 1, d3))


def _inproj_kernel(x_ref, xh_ref, sc_ref, sh_ref, ng_ref, wm_ref, wba_ref, wmg_ref, gq_ref, gk_ref,
                   grp_ref, cw_ref, alog_ref, dtb_ref,
                   q_ref, k_ref, vt_ref, z_ref, dq_ref, dk_ref, dkt_ref, dv_ref, dz_ref,
                   gc_ref, gr_ref, msb_ref, mdn_ref, *, tm, s_tiles):
    t = pl.program_id(0)
    gmod = ng_ref[...] * (1.0 + sc_ref[0])
    shift = sh_ref[0]

    def norm_mod(xv):
        ms = jnp.mean(xv * xv, axis=-1, keepdims=True)
        return (xv * lax.rsqrt(ms + EPS) * gmod + shift).astype(BF16)

    hb = norm_mod(x_ref[...])
    hh = norm_mod(xh_ref[...])
    seq_start = (t % s_tiles) == 0
    cw = cw_ref[...]
    d_model = msb_ref.shape[1]

    def main(g):
        return wm_ref[:, g * SB_WIDTH:(g + 1) * SB_WIDTH]

    def after(token):
        return hb if token is None else _tied(hb, token)

    def token_of(val):
        return val[:F32_SUBLANES, :LANES].astype(F32)

    def head_rms(p, gain):
        sq = (p * p).astype(BF16)
        ms = jnp.concatenate([_dot(sq[:, a:a + MXU_TILE], grp_ref[...])
                              for a in range(0, SB_WIDTH, MXU_TILE)], axis=1)
        return p * lax.rsqrt(ms + EPS) * gain

    def store_transposed(ref, val):
        for a in range(tm // TR_BLOCK):
            ref[0, a] = val[a * TR_BLOCK:(a + 1) * TR_BLOCK, :].T.astype(BF16)

    row = lax.broadcasted_iota(jnp.int32, (F32_SUBLANES, 1), 0)

    def conv_silu(p_and_halo, slot):
        p, ph = p_and_halo
        taps = cw[:, slot * DN_WIDTH:(slot + 1) * DN_WIDTH]
        acc = p * taps[CONV_WIDTH - 1:CONV_WIDTH, :]
        for back in range(1, CONV_WIDTH):
            shifted = pltpu.roll(p, back, axis=0)
            first = jnp.where(row < back, pltpu.roll(ph, back, axis=0), shifted[:F32_SUBLANES, :])
            src = jnp.concatenate([first, shifted[F32_SUBLANES:, :]], axis=0)
            acc = acc + src * taps[CONV_WIDTH - 1 - back:CONV_WIDTH - back, :]
        return _silu(acc)

    def l2n(y, scale):
        outs = []
        for h in range(DN_HEADS):
            yh = y[:, h * DN_HEAD_DIM:(h + 1) * DN_HEAD_DIM]
            ss = jnp.sum(yh * yh, axis=-1, keepdims=True)
            outs.append(yh * (lax.rsqrt(ss + EPS) * scale))
        return jnp.concatenate(outs, axis=1)

    def gates(pba):
        beta = _sigmoid(pba)
        ld = -jnp.exp(alog_ref[...]) * _softplus(pba + dtb_ref[...])
        lane = lax.broadcasted_iota(jnp.int32, (tm, LANES), 1)
        ld = jnp.where((lane >= DN_HEADS) & (lane < 3 * DN_HEADS), ld, 0.0)
        ri = lax.broadcasted_iota(jnp.int32, (tm, tm), 0)
        ci = lax.broadcasted_iota(jnp.int32, (tm, tm), 1)
        same = _block_of(ri, DN_CHUNK) == _block_of(ci, DN_CHUNK)
        lower = jnp.where(same & (ci <= ri), 1.0, 0.0).astype(BF16)
        wide = _dot(lower, jnp.concatenate(_split3(ld), axis=1))
        gcum = wide[:, 0:LANES] + wide[:, LANES:2 * LANES] + wide[:, 2 * LANES:]
        gtot = jnp.concatenate(
            [jnp.broadcast_to(gcum[c * DN_CHUNK + DN_CHUNK - 1:(c + 1) * DN_CHUNK, :], (DN_CHUNK, LANES))
             for c in range(tm // DN_CHUNK)], axis=0)
        g = jnp.where(lane < DN_HEADS, beta, jnp.where(lane < 2 * DN_HEADS, gcum, gtot))
        gc_ref[...] = g
        gr_ref[0] = g.T[0:GATE_ROWS, :]
        return g

    def put(ref, fn):
        def epilogue(p):
            val = fn(p)
            ref[...] = val.astype(ref.dtype)
            return token_of(val)
        return epilogue

    def project(weights):
        return lambda token: _dot(after(token), weights())

    def project_with_halo(g):
        def run(token):
            res = _dot(jnp.concatenate([hh, after(token)], axis=0), main(g))
            halo = res[HALO_ROWS - F32_SUBLANES:HALO_ROWS, :]
            return res[HALO_ROWS:, :], jnp.where(seq_start, 0.0, halo)
        return run

    def value_t(p):
        store_transposed(vt_ref, p)
        return token_of(p)

    def delta_k(p_and_halo):
        yk = l2n(conv_silu(p_and_halo, 1), 1.0)
        dk_ref[...] = yk.astype(BF16)
        store_transposed(dkt_ref, yk)
        return token_of(yk)

    stages = [
        (project_with_halo(4), put(dq_ref, lambda ph: l2n(conv_silu(ph, 0), DN_HEAD_DIM ** -0.5))),
        (project(lambda: wba_ref[...]), lambda p: token_of(gates(p))),
        (project(lambda: main(0)), put(q_ref, lambda p: head_rms(p, gq_ref[...]))),
        (project_with_halo(5), delta_k),
        (project(lambda: main(1)), put(k_ref, lambda p: head_rms(p, gk_ref[...]))),
        (project_with_halo(6), put(dv_ref, lambda ph: conv_silu(ph, 2))),
        (project(lambda: main(2)), value_t),
        (project(lambda: wmg_ref[:, 0:d_model]), put(msb_ref, _sigmoid)),
        (project(lambda: main(3)), put(z_ref, _silu)),
        (project(lambda: wmg_ref[:, d_model:2 * d_model]), put(mdn_ref, _sigmoid)),
        (project(lambda: main(7)), put(dz_ref, _silu)),
    ]
    tokens, pending = [], [stages[g][0](None) for g in range(PIPE_DEPTH)]
    for index in range(len(stages)):
        tokens.append(stages[index][1](pending[index]))
        ahead = index + PIPE_DEPTH
        if ahead < len(stages):
            pending.append(stages[ahead][0](tokens[index]))


def _inproj(x2, scale, shift, norm_g, wm, wba, wmg, gq, gk, grp, cw, alog, dtb, *, batch, seq):
    n, d = x2.shape
    tm = TOKEN_TILE
    s_tiles = seq // tm
    n_sub = tm // TR_BLOCK
    n_blk = seq // TR_BLOCK
    tile = lambda w: pl.BlockSpec((tm, w), lambda t: (t, 0))
    full = lambda a: pl.BlockSpec(a.shape, lambda t: (0,) * a.ndim, pipeline_mode=pl.Buffered(1))
    per_batch = pl.BlockSpec((1, 1, d), lambda t: (t // s_tiles, 0, 0))
    tr_spec = pl.BlockSpec((1, n_sub, SB_WIDTH, TR_BLOCK), lambda t: (t // s_tiles, t % s_tiles, 0, 0))
    tr_shape = jax.ShapeDtypeStruct((batch, n_blk, SB_WIDTH, TR_BLOCK), BF16)
    tok = lambda w, dt=BF16: jax.ShapeDtypeStruct((n, w), dt)
    out_shape = (
        tok(SB_WIDTH), tok(SB_WIDTH), tr_shape, tok(SB_WIDTH),
        tok(DN_WIDTH), tok(DN_WIDTH), tr_shape, tok(DN_WIDTH), tok(DN_WIDTH),
        tok(LANES, F32),
        jax.ShapeDtypeStruct((batch, GATE_ROWS, seq), F32),
        tok(d), tok(d),
    )
    out_specs = (
        tile(SB_WIDTH), tile(SB_WIDTH), tr_spec, tile(SB_WIDTH),
        tile(DN_WIDTH), tile(DN_WIDTH), tr_spec, tile(DN_WIDTH), tile(DN_WIDTH),
        tile(LANES),
        pl.BlockSpec((1, GATE_ROWS, tm), lambda t: (t // s_tiles, 0, t % s_tiles)),
        tile(d), tile(d),
    )
    halo_blocks = tm // HALO_ROWS
    in_specs = [
        tile(d),
        pl.BlockSpec((HALO_ROWS, d), lambda t: (jnp.maximum(t * halo_blocks - 1, 0), 0)),
        per_batch, per_batch, full(norm_g), full(wm), full(wba), full(wmg), full(gq), full(gk),
        full(grp), full(cw), full(alog), full(dtb),
    ]
    return pl.pallas_call(
        functools.partial(_inproj_kernel, tm=tm, s_tiles=s_tiles),
        out_shape=out_shape,
        grid=(n // tm,),
        in_specs=in_specs,
        out_specs=out_specs,
        compiler_params=pltpu.CompilerParams(
            dimension_semantics=("arbitrary",), vmem_limit_bytes=VMEM_LIMIT),
        name="inproj",
    )(x2, x2, scale, shift, norm_g, wm, wba, wmg, gq, gk, grp, cw, alog, dtb)


def _attn_kernel(q_ref, k_ref, vt_ref, z_ref, o_ref, *, tq, tk, reps):
    def one(r, carry):
        rows = pl.ds(pl.multiple_of(r * tq, tq), tq)
        _attn_block(pl.program_id(2) * reps + r, q_ref.at[rows], k_ref, vt_ref, z_ref.at[rows],
                    o_ref.at[rows], tq=tq, tk=tk)
        return carry

    lax.fori_loop(0, reps, one, 0)


def _attn_block(i, q_ref, k_ref, vt_ref, z_ref, o_ref, *, tq, tk):
    width = q_ref.shape[1]
    heads = width // SB_HEAD_DIM
    ks = lax.broadcasted_iota(jnp.int32, (tk, tq), 0)
    qt = lax.broadcasted_iota(jnp.int32, (tk, tq), 1)
    causal = ks < qt
    ur = lax.broadcasted_iota(jnp.int32, (tk, tk), 0)
    uc = lax.broadcasted_iota(jnp.int32, (tk, tk), 1)
    upper = jnp.where(uc > ur, 1.0, 0.0).astype(BF16)
    lane = lax.broadcasted_iota(jnp.int32, (tq, width), 1)
    q = q_ref[...]
    qhs = [jnp.where(_block_of(lane, SB_HEAD_DIM) == hh, q, jnp.zeros_like(q)) for hh in range(heads)]

    def scores(units):
        st = [_dot_nt(k_ref[0, pl.ds(pl.multiple_of(j * tk, tk), tk), :], qh) for j, qh, _ in units]
        sp = [jnp.maximum(s, 0.0) + jnp.log(1.0 + jnp.exp2(_neg_abs(s))) * LOG2E for s in st]
        spm = [p if u[2] is None else jnp.where(u[2], p, 0.0) for p, u in zip(sp, units)]
        later = [_dot(upper, m.astype(BF16)) for m in spm]
        base = [s - p - l for s, p, l in zip(st, sp, later)]
        sums = [l[0:1, :] + m[0:1, :] for l, m in zip(later, spm)]
        return base, sums

    def weights(base, carry, mask):
        w = jnp.exp2(base - carry)
        if mask is not None:
            w = jnp.where(mask, w, 0.0)
        return w.astype(BF16)

    def values(j, hh):
        return vt_ref[0, j, hh * SB_HEAD_DIM:(hh + 1) * SB_HEAD_DIM, :]

    j_hi = 2 * i + 1
    j_lo = 2 * i
    causal_hi = causal[:, :tk]

    def diagonal_units():
        units = []
        for hh in range(heads):
            units += [(j_hi, qhs[hh][tk:, :], causal_hi), (j_lo, qhs[hh], causal)]
        return units

    def diagonal_weights(base, sums):
        w_hi, w_lo, carry = [], [], []
        for hh in range(heads):
            carry_hi = jnp.concatenate([jnp.zeros((1, tk), F32), sums[2 * hh]], axis=1)
            w_hi.append(weights(base[2 * hh], 0.0, causal_hi))
            w_lo.append(weights(base[2 * hh + 1], carry_hi, causal))
            carry.append(carry_hi + sums[2 * hh + 1])
        return w_hi, w_lo, carry

    def pair_units(j):
        units = []
        for hh in range(heads):
            units += [(j, qhs[hh], None), (j - 1, qhs[hh], None)]
        return units

    def pair_weights(base, sums, carry_in):
        ws, carry_out = [], []
        for hh in range(heads):
            w_near = weights(base[2 * hh], carry_in[hh], None)
            carry = carry_in[hh] + sums[2 * hh]
            w_far = weights(base[2 * hh + 1], carry, None)
            ws.append(jnp.concatenate([w_near, w_far], axis=0))
            carry_out.append(carry + sums[2 * hh + 1])
        return ws, carry_out

    def pair_values(j, hh):
        return jnp.concatenate([values(j, hh), values(j - 1, hh)], axis=1)

    def widen(acc_hi):
        return jnp.concatenate([jnp.zeros((SB_HEAD_DIM, tk), F32), acc_hi], axis=1)

    def finish(acc):
        o = jnp.concatenate(acc, axis=0).astype(BF16).T
        o_ref[...] = (o.astype(F32) * z_ref[...].astype(F32)).astype(BF16)

    @pl.when(i == 0)
    def _():
        base, sums = scores(diagonal_units())
        w_hi, w_lo, _ = diagonal_weights(base, sums)
        acc_hi = [_dot(values(j_hi, hh), w_hi[hh]) for hh in range(heads)]
        acc_lo = [_dot(values(j_lo, hh), w_lo[hh]) for hh in range(heads)]
        finish([lo + widen(hi) for lo, hi in zip(acc_lo, acc_hi)])

    @pl.when(i > 0)
    def _():
        n_diag = 2 * heads
        j_c, j_d = 2 * i - 1, 2 * i - 2
        near = [(j_c, qhs[hh], None) for hh in range(heads)]
        far_lower = [(j_d, qhs[hh][:tk, :], None) for hh in range(heads)]
        base, sums = scores(diagonal_units() + near + far_lower)
        w_hi, w_lo, carry = diagonal_weights(base[:n_diag], sums[:n_diag])
        base_c, sums_c = base[n_diag:n_diag + heads], sums[n_diag:n_diag + heads]
        base_d, sums_d = base[n_diag + heads:], sums[n_diag + heads:]
        w_c = [weights(b, c, None) for b, c in zip(base_c, carry)]
        carry = [c + s for c, s in zip(carry, sums_c)]
        w_d = [weights(b, c[:, :tk], None) for b, c in zip(base_d, carry)]
        carry_lower = [c[:, :tk] + s for c, s in zip(carry, sums_d)]
        carry_upper = [c[:, tk:] for c in carry]
        acc_hi = [_dot(values(j_hi, hh), w_hi[hh]) for hh in range(heads)]
        acc_lower = [_dot(values(j_d, hh), w_d[hh]) for hh in range(heads)]
        acc_rest = [_dot(jnp.concatenate([values(j_lo, hh), values(j_c, hh)], axis=1),
                         jnp.concatenate([w_lo[hh], w_c[hh]], axis=0)) for hh in range(heads)]
        acc = [rest + jnp.concatenate([lower, hi], axis=1)
               for rest, lower, hi in zip(acc_rest, acc_lower, acc_hi)]

        def least(carry):
            return jnp.min(functools.reduce(jnp.minimum, carry))

        def upper_far(carry_upper, acc, _):
            base, sums = scores([(j_d, qhs[hh][tk:, :], None) for hh in range(heads)])
            ws = [weights(b, c, None) for b, c in zip(base, carry_upper)]
            extra = [_dot(values(j_d, hh), ws[hh]) for hh in range(heads)]
            carry_upper = [c + s for c, s in zip(carry_upper, sums)]
            return carry_upper, [a + widen(e) for a, e in zip(acc, extra)], least(carry_upper)

        least_lower, least_upper = least(carry_lower), least(carry_upper)
        carry_upper, acc, least_upper = lax.cond(least_upper < EXIT_LOG2, upper_far,
                                                 lambda c, a, m: (c, a, m), carry_upper, acc, least_upper)
        carry = [jnp.concatenate([lo, up], axis=1) for lo, up in zip(carry_lower, carry_upper)]

        def cond(st):
            return (st[0] >= 1) & (st[1] < EXIT_LOG2)

        def body(st):
            j, carry, acc = st[0], list(st[2:2 + heads]), list(st[2 + heads:])
            base, sums = scores(pair_units(j))
            ws, carry = pair_weights(base, sums, carry)
            acc = [a + _dot(pair_values(j, hh), ws[hh]) for hh, a in enumerate(acc)]
            return (j - 2, least(carry), *carry, *acc)

        final = lax.while_loop(cond, body, (2 * i - 3, jnp.minimum(least_lower, least_upper), *carry, *acc))
        finish(list(final[2 + heads:]))


def _attention(q, k, vt, z, *, batch, seq):
    tq, tk = ATTN_Q, TR_BLOCK
    width = ATTN_HEADS * SB_HEAD_DIM
    assert tq == 2 * tk and width % LANES == 0 and SB_WIDTH % width == 0
    reps = min(ATTN_REPS, seq // tq)
    assert seq % (tq * reps) == 0
    n_steps = seq // (tq * reps)
    groups = SB_WIDTH // width
    k3 = k.reshape(batch, seq, SB_WIDTH)
    qspec = pl.BlockSpec((tq * reps, width), lambda b, p, i: (b * n_steps + i, p))
    return pl.pallas_call(
        functools.partial(_attn_kernel, tq=tq, tk=tk, reps=reps),
        out_shape=jax.ShapeDtypeStruct(q.shape, BF16),
        grid=(batch, groups, n_steps),
        in_specs=[
            qspec,
            pl.BlockSpec((1, seq, width), lambda b, p, i: (b, 0, p)),
            pl.BlockSpec((1, seq // tk, width, tk), lambda b, p, i: (b, 0, p, 0)),
            qspec,
        ],
        out_specs=qspec,
        compiler_params=pltpu.CompilerParams(
            dimension_semantics=("arbitrary", "arbitrary", "arbitrary"), vmem_limit_bytes=VMEM_LIMIT),
        name="sb_attention",
    )(q, k3, vt, z)


def _inverse_minus_identity(ms, ri, ci, size):
    mm = lambda a, b: _dot(a.astype(BF16), b.astype(BF16))
    each = lambda f, *lists: [f(*xs) for xs in zip(*lists)]
    base = _block_of(ri, INV_BASE) == _block_of(ci, INV_BASE)
    md = each(lambda m: jnp.where(base, m, 0.0), ms)
    m2 = each(lambda a: mm(a, a), md)
    m4 = each(lambda a: mm(a, a), m2)
    m8 = each(lambda a: mm(a, a), m4)
    n = each(lambda a, b: b - a - mm(a, b), md, m2)
    n = each(lambda a, b: a + b + mm(a, b), n, m4)
    n = each(lambda a, b: a + b + mm(a, b), n, m8)
    width = INV_BASE
    while width < size:
        inner = _block_of(ri, width) == _block_of(ci, width)
        outer = _block_of(ri, 2 * width) == _block_of(ci, 2 * width)
        below = outer & jnp.logical_not(inner)
        mo = each(lambda m: jnp.where(below, m, 0.0), ms)
        x = each(lambda a, b: b + mm(a, b), n, mo)
        n = each(lambda a, b: a - b - mm(b, a), n, x)
        width *= 2
    return n


def _dn_kernel(q_ref, k_ref, kt_ref, v_ref, z_ref, gc_ref, gr_ref, ng_ref, o_ref, state_ref, *, blk):
    @pl.when(pl.program_id(1) == 0)
    def _():
        state_ref[...] = jnp.zeros_like(state_ref)

    cs = DN_CHUNK
    ri = lax.broadcasted_iota(jnp.int32, (cs, cs), 0)
    ci = lax.broadcasted_iota(jnp.int32, (cs, cs), 1)
    tril = ci <= ri
    strict = ci < ri
    gates = gc_ref[...]
    grows = gr_ref[0]
    each = lambda f, *lists: [f(*xs) for xs in zip(*lists)]
    units = [(c, h) for c in range(blk // cs) for h in range(DN_HEADS)]
    rows = lambda c: slice(c * cs, (c + 1) * cs)
    cols = lambda h: slice(h * DN_HEAD_DIM, (h + 1) * DN_HEAD_DIM)
    q = [q_ref[rows(c), cols(h)] for c, h in units]
    k = [k_ref[rows(c), cols(h)] for c, h in units]
    beta = [gates[rows(c), h:h + 1] for c, h in units]
    g = [gates[rows(c), DN_HEADS + h:DN_HEADS + h + 1] for c, h in units]
    g_row = [grows[DN_HEADS + h:DN_HEADS + h + 1, rows(c)] for c, h in units]
    gt_row = [grows[2 * DN_HEADS + h:2 * DN_HEADS + h + 1, rows(c)] for c, h in units]
    eg = each(jnp.exp, g)
    kb = each(lambda a, b: a.astype(F32) * b, k, beta)
    vb = [v_ref[rows(c), cols(h)].astype(F32) * b for (c, h), b in zip(units, beta)]
    decay = each(lambda a, b: jnp.exp(jnp.where(tril, a - b, NEG_BIG)), g, g_row)
    m = each(lambda a, b, d: jnp.where(strict, _dot_nt(a.astype(BF16), b) * d, 0.0), kb, k, decay)
    n = each(lambda a: a.astype(BF16), _inverse_minus_identity(m, ri, ci, cs))
    uw = each(lambda a, b, e: jnp.concatenate([a, b * e], axis=1), vb, kb, eg)
    uw = each(lambda a, b: b + _dot(a, b.astype(BF16)), n, uw)
    intra = each(lambda a, b, d: (_dot_nt(a, b) * d).astype(BF16), q, k, decay)
    ktd = [(kt_ref[0, c, cols(h), :].astype(F32) * jnp.exp(t - r)).astype(BF16)
           for (c, h), t, r in zip(units, gt_row, g_row)]
    carry = each(lambda t: jnp.exp(t[:, 0:1]), gt_row)
    uwb = each(lambda a: a.astype(BF16), uw)
    kuw = each(_dot, ktd, uwb)
    iuw = each(_dot, intra, uwb)
    lhs = each(lambda a, b, c, e: jnp.concatenate(
        [a[:, DN_HEAD_DIM:], c.astype(F32) * e - b[:, DN_HEAD_DIM:]], axis=0).astype(BF16),
        kuw, iuw, q, eg)

    state = [state_ref[h] for h in range(DN_HEADS)]
    outs = []
    for c in range(blk // cs):
        of = lambda xs: xs[c * DN_HEADS:(c + 1) * DN_HEADS]
        prod = each(lambda a, s: _dot(a, s.astype(BF16)), of(lhs), state)
        outs.append(each(lambda p, b: p[DN_HEAD_DIM:] + b[:, :DN_HEAD_DIM], prod, of(iuw)))
        state = each(lambda s, e, p, a: s * e - p[:DN_HEAD_DIM] + a[:, :DN_HEAD_DIM],
                     state, of(carry), prod, of(kuw))
    for h in range(DN_HEADS):
        state_ref[h] = state[h]
        o = jnp.concatenate([chunk_out[h] for chunk_out in outs], axis=0)
        ms = jnp.mean(o * o, axis=-1, keepdims=True)
        on = o * lax.rsqrt(ms + EPS) * ng_ref[...]
        o_ref[:, cols(h)] = (on * z_ref[:, cols(h)].astype(F32)).astype(BF16)


def _deltanet(dq, dk, dkt, dv, dz, gc, gr, ng, *, batch, seq):
    blk = DN_BLOCK
    n_blk = seq // blk
    tok = pl.BlockSpec((blk, DN_WIDTH), lambda b, c: (b * n_blk + c, 0))
    return pl.pallas_call(
        functools.partial(_dn_kernel, blk=blk),
        out_shape=jax.ShapeDtypeStruct(dq.shape, BF16),
        grid=(batch, n_blk),
        in_specs=[
            tok, tok,
            pl.BlockSpec((1, blk // TR_BLOCK, DN_WIDTH, TR_BLOCK), lambda b, c: (b, c, 0, 0)),
            tok, tok,
            pl.BlockSpec((blk, LANES), lambda b, c: (b * n_blk + c, 0)),
            pl.BlockSpec((1, GATE_ROWS, blk), lambda b, c: (b, 0, c)),
            pl.BlockSpec((1, DN_HEAD_DIM), lambda b, c: (0, 0)),
        ],
        out_specs=tok,
        scratch_shapes=[pltpu.VMEM((DN_HEADS, DN_HEAD_DIM, DN_HEAD_DIM), F32)],
        compiler_params=pltpu.CompilerParams(
            dimension_semantics=("arbitrary", "arbitrary"), vmem_limit_bytes=VMEM_LIMIT),
        name="deltanet",
    )(dq, dk, dkt, dv, dz, gc, gr, ng)


def _outproj_kernel(x_ref, osb_ref, odn_ref, msb_ref, mdn_ref, gate_ref, wsb_ref, wdn_ref, wo_ref, o_ref):
    ysb = _dot(osb_ref[...], wsb_ref[...])
    ydn = _dot(odn_ref[...], wdn_ref[...])
    y = msb_ref[...].astype(F32) * ysb + mdn_ref[...].astype(F32) * ydn
    out = _dot(y.astype(BF16), wo_ref[...])
    o_ref[...] = x_ref[...] + gate_ref[0] * out


def _outproj(x2, osb, odn, msb, mdn, gate, wsb, wdn, wo, *, seq):
    n, d = x2.shape
    tm = OUT_TILE
    assert seq % tm == 0
    s_tiles = seq // tm
    tile = lambda w: pl.BlockSpec((tm, w), lambda t: (t, 0))
    full = lambda a: pl.BlockSpec(a.shape, lambda t: (0,) * a.ndim, pipeline_mode=pl.Buffered(1))
    return pl.pallas_call(
        _outproj_kernel,
        out_shape=jax.ShapeDtypeStruct((n, d), F32),
        grid=(n // tm,),
        in_specs=[tile(d), tile(SB_WIDTH), tile(DN_WIDTH), tile(d), tile(d),
                  pl.BlockSpec((1, 1, d), lambda t: (t // s_tiles, 0, 0)),
                  full(wsb), full(wdn), full(wo)],
        out_specs=tile(d),
        compiler_params=pltpu.CompilerParams(
            dimension_semantics=("arbitrary",), vmem_limit_bytes=VMEM_LIMIT),
        name="outproj",
    )(x2, osb, odn, msb, mdn, gate, wsb, wdn, wo)


def _layer(x2, mod, norm_g, w_in, sb_q_g, sb_k_g, conv_w, dn_a_log, dn_dt_bias, dn_norm_g,
           w_branch_sb, w_branch_dn, w_out, *, batch, seq):
    d = x2.shape[1]
    shift = mod[:batch, 0:d].reshape(batch, 1, d)
    scale = mod[:batch, d:2 * d].reshape(batch, 1, d)
    gate = mod[:batch, 2 * d:3 * d].reshape(batch, 1, d)

    main_cols = 4 * SB_WIDTH + 4 * DN_WIDTH
    wm = w_in[:, :main_cols].astype(BF16)
    w_b = w_in[:, main_cols:main_cols + DN_HEADS]
    w_a = w_in[:, main_cols + DN_HEADS:main_cols + 2 * DN_HEADS]
    pad = jnp.zeros((d, LANES - 3 * DN_HEADS), F32)
    wba = jnp.concatenate([w_b, w_a, w_a, pad], axis=1).astype(BF16)
    wmg = w_in[:, main_cols + 2 * DN_HEADS:].astype(BF16)

    def gate_lanes(v):
        z = jnp.zeros((DN_HEADS,), F32)
        return jnp.concatenate([z, v, v, jnp.zeros((LANES - 3 * DN_HEADS,), F32)]).reshape(1, LANES)

    gq = (jnp.tile(sb_q_g, SB_HEADS) * (LOG2E * SB_HEAD_DIM ** -0.5)).reshape(1, SB_WIDTH)
    gk = jnp.tile(sb_k_g, SB_HEADS).reshape(1, SB_WIDTH)
    head_of = jnp.arange(MXU_TILE) // SB_HEAD_DIM
    grp = jnp.where(head_of[:, None] == head_of[None, :], 1.0 / SB_HEAD_DIM, 0.0).astype(BF16)

    (q, k, vt, z, dq, dk, dkt, dv, dz, gc, gr, msb, mdn) = _inproj(
        x2, scale, shift, norm_g.reshape(1, d), wm, wba, wmg, gq, gk, grp, conv_w,
        gate_lanes(dn_a_log), gate_lanes(dn_dt_bias), batch=batch, seq=seq)
    osb = _attention(q, k, vt, z, batch=batch, seq=seq)
    odn = _deltanet(dq, dk, dkt, dv, dz, gc, gr, dn_norm_g.reshape(1, DN_HEAD_DIM), batch=batch, seq=seq)
    return _outproj(x2, osb, odn, msb, mdn, gate, w_branch_sb.astype(BF16), w_branch_dn.astype(BF16),
                    w_out.astype(BF16), seq=seq)


def kernel(x, c, ada_w, ada_b, norm_g, w_in, sb_q_g, sb_k_g, conv_w, dn_a_log, dn_dt_bias, dn_norm_g,
           w_branch_sb, w_branch_dn, w_out):
    batch, seq, d = x.shape
    depth = ada_w.shape[0]
    assert seq % TOKEN_TILE == 0 and seq % DN_BLOCK == 0 and seq % ATTN_Q == 0
    assert DN_CHUNK == TR_BLOCK and DN_BLOCK % DN_CHUNK == 0 and TOKEN_TILE % DN_CHUNK == 0
    rows = -(-batch // ADALN_ROWS) * ADALN_ROWS
    c_pad = jnp.concatenate([c, jnp.zeros((rows - batch, d), c.dtype)], axis=0) if batch < rows else c
    mod = _adaln(c_pad, ada_w, ada_b)
    x2 = x.reshape(batch * seq, d)
    for l in range(depth):
        x2 = _layer(x2, mod[l], norm_g[l], w_in[l], sb_q_g[l], sb_k_g[l], conv_w[l], dn_a_log[l],
                    dn_dt_bias[l], dn_norm_g[l], w_branch_sb[l], w_branch_dn[l], w_out[l],
                    batch=batch, seq=seq)
    return x2.reshape(batch, seq, d)
```

```python
import functools

import jax
import jax.numpy as jnp
from jax import lax
from jax.experimental import pallas as pl
from jax.experimental.pallas import tpu as pltpu

F32 = jnp.float32
BF16 = jnp.bfloat16

EPS = 1e-6
SB_HEADS = 8
SB_HEAD_DIM = 64
SB_WIDTH = SB_HEADS * SB_HEAD_DIM
DN_HEADS = 4
DN_HEAD_DIM = 128
DN_WIDTH = DN_HEADS * DN_HEAD_DIM
CONV_WIDTH = 4

LANES = 128
MXU_TILE = 256
F32_SUBLANES = 8
HALO_ROWS = 16
GATE_ROWS = 16
TOKEN_TILE = 1024
OUT_TILE = 1024
PIPE_DEPTH = 3
ATTN_Q = 256
ATTN_HEADS = 4
ATTN_REPS = 16
TR_BLOCK = 128
DN_BLOCK = 1024
DN_CHUNK = 128
INV_BASE = 16
V7X_VMEM_BYTES = 64 * 1024 * 1024
VMEM_LIMIT = V7X_VMEM_BYTES * 15 // 16
ADALN_ROWS = 16
NEG_BIG = -1e30
LOG2E = 1.4426950408889634
EXIT_LOG2 = 152.0


def _dot(a, b):
    return jnp.dot(a, b, preferred_element_type=F32)


def _dot_nt(a, b):
    return lax.dot_general(a, b, (((1,), (1,)), ((), ())), preferred_element_type=F32)


def _sigmoid(x):
    return 0.5 * jnp.tanh(0.5 * x) + 0.5


def _silu(x):
    h = 0.5 * x
    return h + h * jnp.tanh(h)


def _neg_abs(x):
    bits = lax.bitcast_convert_type(x, jnp.uint32) | jnp.uint32(0x80000000)
    return lax.bitcast_convert_type(bits, F32)


def _softplus(x):
    return jnp.maximum(x, 0.0) + jnp.log1p(jnp.exp(-jnp.abs(x)))


def _block_of(idx, size):
    assert size & (size - 1) == 0
    return jnp.right_shift(idx, size.bit_length() - 1)


def _tied(operand, token):
    bits = lax.bitcast_convert_type(token, jnp.uint32)
    zero = jnp.right_shift(jnp.right_shift(bits, 16), 16).astype(F32)
    zero = jnp.concatenate([zero, zero], axis=0).astype(BF16)
    rows, lanes = zero.shape
    first = jnp.concatenate([operand[:rows, :lanes] + zero, operand[:rows, lanes:]], axis=1)
    return jnp.concatenate([first, operand[rows:, :]], axis=0)


def _split3(x):
    hi = x.astype(BF16)
    r = x - hi.astype(F32)
    mid = r.astype(BF16)
    lo = (r - mid.astype(F32)).astype(BF16)
    return hi, mid, lo


def _adaln_kernel(c_ref, w_ref, b_ref, o_ref):
    c = c_ref[...]
    a = c * _sigmoid(c)
    w = w_ref[0]
    a_hi = a.astype(BF16)
    a_lo = (a - a_hi.astype(F32)).astype(BF16)
    w_hi = w.astype(BF16)
    w_lo = (w - w_hi.astype(F32)).astype(BF16)
    o_ref[0] = _dot(a_hi, w_hi) + _dot(a_hi, w_lo) + _dot(a_lo, w_hi) + b_ref[0]


def _adaln(c_pad, ada_w, ada_b):
    depth, d, d3 = ada_w.shape
    rows = c_pad.shape[0]
    tn = 512
    return pl.pallas_call(
        _adaln_kernel,
        out_shape=jax.ShapeDtypeStruct((depth, rows, d3), F32),
        grid=(depth, d3 // tn),
        in_specs=[
            pl.BlockSpec((rows, d), lambda l, j: (0, 0)),
            pl.BlockSpec((1, d, tn), lambda l, j: (l, 0, j)),
            pl.BlockSpec((1, 1, tn), lambda l, j: (l, 0, j)),
        ],
        out_specs=pl.BlockSpec((1, rows, tn), lambda l, j: (l, 0, j)),
        compiler_params=pltpu.CompilerParams(
            dimension_semantics=("arbitrary", "arbitrary"), vmem_limit_bytes=VMEM_LIMIT),
        name="adaln",
    )(c_pad, ada_w, ada_b.reshape(depth, 1, d3))


def _inproj_kernel(x_ref, xh_ref, sc_ref, sh_ref, ng_ref, wm_ref, wba_ref, wmg_ref, gq_ref, gk_ref,
                   grp_ref, cw_ref, alog_ref, dtb_ref,
                   q_ref, k_ref, vt_ref, z_ref, dq_ref, dk_ref, dkt_ref, dv_ref, dz_ref,
                   gc_ref, gr_ref, msb_ref, mdn_ref, *, tm, s_tiles):
    t = pl.program_id(0)
    gmod = ng_ref[...] * (1.0 + sc_ref[0])
    shift = sh_ref[0]

    def norm_mod(xv):
        ms = jnp.mean(xv * xv, axis=-1, keepdims=True)
        return (xv * lax.rsqrt(ms + EPS) * gmod + shift).astype(BF16)

    hb = norm_mod(x_ref[...])
    hh = norm_mod(xh_ref[...])
    seq_start = (t % s_tiles) == 0
    cw = cw_ref[...]
    d_model = msb_ref.shape[1]

    def main(g):
        return wm_ref[:, g * SB_WIDTH:(g + 1) * SB_WIDTH]

    def after(token):
        return hb if token is None else _tied(hb, token)

    def token_of(val):
        return val[:F32_SUBLANES, :LANES].astype(F32)

    def head_rms(p, gain):
        sq = (p * p).astype(BF16)
        ms = jnp.concatenate([_dot(sq[:, a:a + MXU_TILE], grp_ref[...])
                              for a in range(0, SB_WIDTH, MXU_TILE)], axis=1)
        return p * lax.rsqrt(ms + EPS) * gain

    def store_transposed(ref, val):
        for a in range(tm // TR_BLOCK):
            ref[0, a] = val[a * TR_BLOCK:(a + 1) * TR_BLOCK, :].T.astype(BF16)

    row = lax.broadcasted_iota(jnp.int32, (F32_SUBLANES, 1), 0)

    def conv_silu(p_and_halo, slot):
        p, ph = p_and_halo
        taps = cw[:, slot * DN_WIDTH:(slot + 1) * DN_WIDTH]
        acc = p * taps[CONV_WIDTH - 1:CONV_WIDTH, :]
        for back in range(1, CONV_WIDTH):
            shifted = pltpu.roll(p, back, axis=0)
            first = jnp.where(row < back, pltpu.roll(ph, back, axis=0), shifted[:F32_SUBLANES, :])
            src = jnp.concatenate([first, shifted[F32_SUBLANES:, :]], axis=0)
            acc = acc + src * taps[CONV_WIDTH - 1 - back:CONV_WIDTH - back, :]
        return _silu(acc)

    def l2n(y, scale):
        outs = []
        for h in range(DN_HEADS):
            yh = y[:, h * DN_HEAD_DIM:(h + 1) * DN_HEAD_DIM]
            ss = jnp.sum(yh * yh, axis=-1, keepdims=True)
            outs.append(yh * (lax.rsqrt(ss + EPS) * scale))
        return jnp.concatenate(outs, axis=1)

    def gates(pba):
        beta = _sigmoid(pba)
        ld = -jnp.exp(alog_ref[...]) * _softplus(pba + dtb_ref[...])
        lane = lax.broadcasted_iota(jnp.int32, (tm, LANES), 1)
        ld = jnp.where((lane >= DN_HEADS) & (lane < 3 * DN_HEADS), ld, 0.0)
        ri = lax.broadcasted_iota(jnp.int32, (tm, tm), 0)
        ci = lax.broadcasted_iota(jnp.int32, (tm, tm), 1)
        same = _block_of(ri, DN_CHUNK) == _block_of(ci, DN_CHUNK)
        lower = jnp.where(same & (ci <= ri), 1.0, 0.0).astype(BF16)
        wide = _dot(lower, jnp.concatenate(_split3(ld), axis=1))
        gcum = wide[:, 0:LANES] + wide[:, LANES:2 * LANES] + wide[:, 2 * LANES:]
        gtot = jnp.concatenate(
            [jnp.broadcast_to(gcum[c * DN_CHUNK + DN_CHUNK - 1:(c + 1) * DN_CHUNK, :], (DN_CHUNK, LANES))
             for c in range(tm // DN_CHUNK)], axis=0)
        g = jnp.where(lane < DN_HEADS, beta, jnp.where(lane < 2 * DN_HEADS, gcum, gtot))
        gc_ref[...] = g
        gr_ref[0] = g.T[0:GATE_ROWS, :]
        return g

    def put(ref, fn):
        def epilogue(p):
            val = fn(p)
            ref[...] = val.astype(ref.dtype)
            return token_of(val)
        return epilogue

    def project(weights):
        return lambda token: _dot(after(token), weights())

    def project_with_halo(g):
        def run(token):
            res = _dot(jnp.concatenate([hh, after(token)], axis=0), main(g))
            halo = res[HALO_ROWS - F32_SUBLANES:HALO_ROWS, :]
            return res[HALO_ROWS:, :], jnp.where(seq_start, 0.0, halo)
        return run

    def value_t(p):
        store_transposed(vt_ref, p)
        return token_of(p)

    def delta_k(p_and_halo):
        yk = l2n(conv_silu(p_and_halo, 1), 1.0)
        dk_ref[...] = yk.astype(BF16)
        store_transposed(dkt_ref, yk)
        return token_of(yk)

    stages = [
        (project_with_halo(4), put(dq_ref, lambda ph: l2n(conv_silu(ph, 0), DN_HEAD_DIM ** -0.5))),
        (project(lambda: wba_ref[...]), lambda p: token_of(gates(p))),
        (project(lambda: main(0)), put(q_ref, lambda p: head_rms(p, gq_ref[...]))),
        (project_with_halo(5), delta_k),
        (project(lambda: main(1)), put(k_ref, lambda p: head_rms(p, gk_ref[...]))),
        (project_with_halo(6), put(dv_ref, lambda ph: conv_silu(ph, 2))),
        (project(lambda: main(2)), value_t),
        (project(lambda: wmg_ref[:, 0:d_model]), put(msb_ref, _sigmoid)),
        (project(lambda: main(3)), put(z_ref, _silu)),
        (project(lambda: wmg_ref[:, d_model:2 * d_model]), put(mdn_ref, _sigmoid)),
        (project(lambda: main(7)), put(dz_ref, _silu)),
    ]
    tokens, pending = [], [stages[g][0](None) for g in range(PIPE_DEPTH)]
    for index in range(len(stages)):
        tokens.append(stages[index][1](pending[index]))
        ahead = index + PIPE_DEPTH
        if ahead < len(stages):
            pending.append(stages[ahead][0](tokens[index]))


def _inproj(x2, scale, shift, norm_g, wm, wba, wmg, gq, gk, grp, cw, alog, dtb, *, batch, seq):
    n, d = x2.shape
    tm = TOKEN_TILE
    s_tiles = seq // tm
    n_sub = tm // TR_BLOCK
    n_blk = seq // TR_BLOCK
    tile = lambda w: pl.BlockSpec((tm, w), lambda t: (t, 0))
    full = lambda a: pl.BlockSpec(a.shape, lambda t: (0,) * a.ndim, pipeline_mode=pl.Buffered(1))
    per_batch = pl.BlockSpec((1, 1, d), lambda t: (t // s_tiles, 0, 0))
    tr_spec = pl.BlockSpec((1, n_sub, SB_WIDTH, TR_BLOCK), lambda t: (t // s_tiles, t % s_tiles, 0, 0))
    tr_shape = jax.ShapeDtypeStruct((batch, n_blk, SB_WIDTH, TR_BLOCK), BF16)
    tok = lambda w, dt=BF16: jax.ShapeDtypeStruct((n, w), dt)
    out_shape = (
        tok(SB_WIDTH), tok(SB_WIDTH), tr_shape, tok(SB_WIDTH),
        tok(DN_WIDTH), tok(DN_WIDTH), tr_shape, tok(DN_WIDTH), tok(DN_WIDTH),
        tok(LANES, F32),
        jax.ShapeDtypeStruct((batch, GATE_ROWS, seq), F32),
        tok(d), tok(d),
    )
    out_specs = (
        tile(SB_WIDTH), tile(SB_WIDTH), tr_spec, tile(SB_WIDTH),
        tile(DN_WIDTH), tile(DN_WIDTH), tr_spec, tile(DN_WIDTH), tile(DN_WIDTH),
        tile(LANES),
        pl.BlockSpec((1, GATE_ROWS, tm), lambda t: (t // s_tiles, 0, t % s_tiles)),
        tile(d), tile(d),
    )
    halo_blocks = tm // HALO_ROWS
    in_specs = [
        tile(d),
        pl.BlockSpec((HALO_ROWS, d), lambda t: (jnp.maximum(t * halo_blocks - 1, 0), 0)),
        per_batch, per_batch, full(norm_g), full(wm), full(wba), full(wmg), full(gq), full(gk),
        full(grp), full(cw), full(alog), full(dtb),
    ]
    return pl.pallas_call(
        functools.partial(_inproj_kernel, tm=tm, s_tiles=s_tiles),
        out_shape=out_shape,
        grid=(n // tm,),
        in_specs=in_specs,
        out_specs=out_specs,
        compiler_params=pltpu.CompilerParams(
            dimension_semantics=("arbitrary",), vmem_limit_bytes=VMEM_LIMIT),
        name="inproj",
    )(x2, x2, scale, shift, norm_g, wm, wba, wmg, gq, gk, grp, cw, alog, dtb)


def _attn_kernel(q_ref, k_ref, vt_ref, z_ref, o_ref, *, tq, tk, reps):
    def one(r, carry):
        rows = pl.ds(pl.multiple_of(r * tq, tq), tq)
        _attn_block(pl.program_id(2) * reps + r, q_ref.at[rows], k_ref, vt_ref, z_ref.at[rows],
                    o_ref.at[rows], tq=tq, tk=tk)
        return carry

    lax.fori_loop(0, reps, one, 0)


def _attn_block(i, q_ref, k_ref, vt_ref, z_ref, o_ref, *, tq, tk):
    width = q_ref.shape[1]
    heads = width // SB_HEAD_DIM
    ks = lax.broadcasted_iota(jnp.int32, (tk, tq), 0)
    qt = lax.broadcasted_iota(jnp.int32, (tk, tq), 1)
    causal = ks < qt
    ur = lax.broadcasted_iota(jnp.int32, (tk, tk), 0)
    uc = lax.broadcasted_iota(jnp.int32, (tk, tk), 1)
    upper = jnp.where(uc > ur, 1.0, 0.0).astype(BF16)
    lane = lax.broadcasted_iota(jnp.int32, (tq, width), 1)
    q = q_ref[...]
    qhs = [jnp.where(_block_of(lane, SB_HEAD_DIM) == hh, q, jnp.zeros_like(q)) for hh in range(heads)]

    def scores(units):
        st = [_dot_nt(k_ref[0, pl.ds(pl.multiple_of(j * tk, tk), tk), :], qh) for j, qh, _ in units]
        sp = [jnp.maximum(s, 0.0) + jnp.log(1.0 + jnp.exp2(_neg_abs(s))) * LOG2E for s in st]
        spm = [p if u[2] is None else jnp.where(u[2], p, 0.0) for p, u in zip(sp, units)]
        later = [_dot(upper, m.astype(BF16)) for m in spm]
        base = [s - p - l for s, p, l in zip(st, sp, later)]
        sums = [l[0:1, :] + m[0:1, :] for l, m in zip(later, spm)]
        return base, sums

    def weights(base, carry, mask):
        w = jnp.exp2(base - carry)
        if mask is not None:
            w = jnp.where(mask, w, 0.0)
        return w.astype(BF16)

    def values(j, hh):
        return vt_ref[0, j, hh * SB_HEAD_DIM:(hh + 1) * SB_HEAD_DIM, :]

    j_hi = 2 * i + 1
    j_lo = 2 * i
    causal_hi = causal[:, :tk]

    def diagonal_units():
        units = []
        for hh in range(heads):
            units += [(j_hi, qhs[hh][tk:, :], causal_hi), (j_lo, qhs[hh], causal)]
        return units

    def diagonal_weights(base, sums):
        w_hi, w_lo, carry = [], [], []
        for hh in range(heads):
            carry_hi = jnp.concatenate([jnp.zeros((1, tk), F32), sums[2 * hh]], axis=1)
            w_hi.append(weights(base[2 * hh], 0.0, causal_hi))
            w_lo.append(weights(base[2 * hh + 1], carry_hi, causal))
            carry.append(carry_hi + sums[2 * hh + 1])
        return w_hi, w_lo, carry

    def pair_units(j):
        units = []
        for hh in range(heads):
            units += [(j, qhs[hh], None), (j - 1, qhs[hh], None)]
        return units

    def pair_weights(base, sums, carry_in):
        ws, carry_out = [], []
        for hh in range(heads):
            w_near = weights(base[2 * hh], carry_in[hh], None)
            carry = carry_in[hh] + sums[2 * hh]
            w_far = weights(base[2 * hh + 1], carry, None)
            ws.append(jnp.concatenate([w_near, w_far], axis=0))
            carry_out.append(carry + sums[2 * hh + 1])
        return ws, carry_out

    def pair_values(j, hh):
        return jnp.concatenate([values(j, hh), values(j - 1, hh)], axis=1)

    def widen(acc_hi):
        return jnp.concatenate([jnp.zeros((SB_HEAD_DIM, tk), F32), acc_hi], axis=1)

    def finish(acc):
        o = jnp.concatenate(acc, axis=0).astype(BF16).T
        o_ref[...] = (o.astype(F32) * z_ref[...].astype(F32)).astype(BF16)

    @pl.when(i == 0)
    def _():
        base, sums = scores(diagonal_units())
        w_hi, w_lo, _ = diagonal_weights(base, sums)
        acc_hi = [_dot(values(j_hi, hh), w_hi[hh]) for hh in range(heads)]
        acc_lo = [_dot(values(j_lo, hh), w_lo[hh]) for hh in range(heads)]
        finish([lo + widen(hi) for lo, hi in zip(acc_lo, acc_hi)])

    @pl.when(i > 0)
    def _():
        n_diag = 2 * heads
        j_c, j_d = 2 * i - 1, 2 * i - 2
        near = [(j_c, qhs[hh], None) for hh in range(heads)]
        far_lower = [(j_d, qhs[hh][:tk, :], None) for hh in range(heads)]
        base, sums = scores(diagonal_units() + near + far_lower)
        w_hi, w_lo, carry = diagonal_weights(base[:n_diag], sums[:n_diag])
        base_c, sums_c = base[n_diag:n_diag + heads], sums[n_diag:n_diag + heads]
        base_d, sums_d = base[n_diag + heads:], sums[n_diag + heads:]
        w_c = [weights(b, c, None) for b, c in zip(base_c, carry)]
        carry = [c + s for c, s in zip(carry, sums_c)]
        w_d = [weights(b, c[:, :tk], None) for b, c in zip(base_d, carry)]
        carry_lower = [c[:, :tk] + s for c, s in zip(carry, sums_d)]
        carry_upper = [c[:, tk:] for c in carry]
        acc_hi = [_dot(values(j_hi, hh), w_hi[hh]) for hh in range(heads)]
        acc_lower = [_dot(values(j_d, hh), w_d[hh]) for hh in range(heads)]
        acc_rest = [_dot(jnp.concatenate([values(j_lo, hh), values(j_c, hh)], axis=1),
                         jnp.concatenate([w_lo[hh], w_c[hh]], axis=0)) for hh in range(heads)]
        acc = [rest + jnp.concatenate([lower, hi], axis=1)
               for rest, lower, hi in zip(acc_rest, acc_lower, acc_hi)]

        def least(carry):
            return jnp.min(functools.reduce(jnp.minimum, carry))

        def upper_far(carry_upper, acc, _):
            base, sums = scores([(j_d, qhs[hh][tk:, :], None) for hh in range(heads)])
            ws = [weights(b, c, None) for b, c in zip(base, carry_upper)]
            extra = [_dot(values(j_d, hh), ws[hh]) for hh in range(heads)]
            carry_upper = [c + s for c, s in zip(carry_upper, sums)]
            return carry_upper, [a + widen(e) for a, e in zip(acc, extra)], least(carry_upper)

        least_lower, least_upper = least(carry_lower), least(carry_upper)
        carry_upper, acc, least_upper = lax.cond(least_upper < EXIT_LOG2, upper_far,
                                                 lambda c, a, m: (c, a, m), carry_upper, acc, least_upper)
        carry = [jnp.concatenate([lo, up], axis=1) for lo, up in zip(carry_lower, carry_upper)]

        def cond(st):
            return (st[0] >= 1) & (st[1] < EXIT_LOG2)

        def body(st):
            j, carry, acc = st[0], list(st[2:2 + heads]), list(st[2 + heads:])
            base, sums = scores(pair_units(j))
            ws, carry = pair_weights(base, sums, carry)
            acc = [a + _dot(pair_values(j, hh), ws[hh]) for hh, a in enumerate(acc)]
            return (j - 2, least(carry), *carry, *acc)

        final = lax.while_loop(cond, body, (2 * i - 3, jnp.minimum(least_lower, least_upper), *carry, *acc))
        finish(list(final[2 + heads:]))


def _attention(q, k, vt, z, *, batch, seq):
    tq, tk = ATTN_Q, TR_BLOCK
    width = ATTN_HEADS * SB_HEAD_DIM
    assert tq == 2 * tk and width % LANES == 0 and SB_WIDTH % width == 0
    reps = min(ATTN_REPS, seq // tq)
    assert seq % (tq * reps) == 0
    n_steps = seq // (tq * reps)
    groups = SB_WIDTH // width
    k3 = k.reshape(batch, seq, SB_WIDTH)
    qspec = pl.BlockSpec((tq * reps, width), lambda b, p, i: (b * n_steps + i, p))
    return pl.pallas_call(
        functools.partial(_attn_kernel, tq=tq, tk=tk, reps=reps),
        out_shape=jax.ShapeDtypeStruct(q.shape, BF16),
        grid=(batch, groups, n_steps),
        in_specs=[
            qspec,
            pl.BlockSpec((1, seq, width), lambda b, p, i: (b, 0, p)),
            pl.BlockSpec((1, seq // tk, width, tk), lambda b, p, i: (b, 0, p, 0)),
            qspec,
        ],
        out_specs=qspec,
        compiler_params=pltpu.CompilerParams(
            dimension_semantics=("arbitrary", "arbitrary", "arbitrary"), vmem_limit_bytes=VMEM_LIMIT),
        name="sb_attention",
    )(q, k3, vt, z)


def _inverse_minus_identity(ms, ri, ci, size):
    mm = lambda a, b: _dot(a.astype(BF16), b.astype(BF16))
    each = lambda f, *lists: [f(*xs) for xs in zip(*lists)]
    base = _block_of(ri, INV_BASE) == _block_of(ci, INV_BASE)
    md = each(lambda m: jnp.where(base, m, 0.0), ms)
    m2 = each(lambda a: mm(a, a), md)
    m4 = each(lambda a: mm(a, a), m2)
    m8 = each(lambda a: mm(a, a), m4)
    n = each(lambda a, b: b - a - mm(a, b), md, m2)
    n = each(lambda a, b: a + b + mm(a, b), n, m4)
    n = each(lambda a, b: a + b + mm(a, b), n, m8)
    width = INV_BASE
    while width < size:
        inner = _block_of(ri, width) == _block_of(ci, width)
        outer = _block_of(ri, 2 * width) == _block_of(ci, 2 * width)
        below = outer & jnp.logical_not(inner)
        mo = each(lambda m: jnp.where(below, m, 0.0), ms)
        x = each(lambda a, b: b + mm(a, b), n, mo)
        n = each(lambda a, b: a - b - mm(b, a), n, x)
        width *= 2
    return n


def _dn_kernel(q_ref, k_ref, kt_ref, v_ref, z_ref, gc_ref, gr_ref, ng_ref, o_ref, state_ref, *, blk):
    @pl.when(pl.program_id(1) == 0)
    def _():
        state_ref[...] = jnp.zeros_like(state_ref)

    cs = DN_CHUNK
    ri = lax.broadcasted_iota(jnp.int32, (cs, cs), 0)
    ci = lax.broadcasted_iota(jnp.int32, (cs, cs), 1)
    tril = ci <= ri
    strict = ci < ri
    gates = gc_ref[...]
    grows = gr_ref[0]
    each = lambda f, *lists: [f(*xs) for xs in zip(*lists)]
    units = [(c, h) for c in range(blk // cs) for h in range(DN_HEADS)]
    rows = lambda c: slice(c * cs, (c + 1) * cs)
    cols = lambda h: slice(h * DN_HEAD_DIM, (h + 1) * DN_HEAD_DIM)
    q = [q_ref[rows(c), cols(h)] for c, h in units]
    k = [k_ref[rows(c), cols(h)] for c, h in units]
    beta = [gates[rows(c), h:h + 1] for c, h in units]
    g = [gates[rows(c), DN_HEADS + h:DN_HEADS + h + 1] for c, h in units]
    g_row = [grows[DN_HEADS + h:DN_HEADS + h + 1, rows(c)] for c, h in units]
    gt_row = [grows[2 * DN_HEADS + h:2 * DN_HEADS + h + 1, rows(c)] for c, h in units]
    eg = each(jnp.exp, g)
    kb = each(lambda a, b: a.astype(F32) * b, k, beta)
    vb = [v_ref[rows(c), cols(h)].astype(F32) * b for (c, h), b in zip(units, beta)]
    decay = each(lambda a, b: jnp.exp(jnp.where(tril, a - b, NEG_BIG)), g, g_row)
    m = each(lambda a, b, d: jnp.where(strict, _dot_nt(a.astype(BF16), b) * d, 0.0), kb, k, decay)
    n = each(lambda a: a.astype(BF16), _inverse_minus_identity(m, ri, ci, cs))
    uw = each(lambda a, b, e: jnp.concatenate([a, b * e], axis=1), vb, kb, eg)
    uw = each(lambda a, b: b + _dot(a, b.astype(BF16)), n, uw)
    intra = each(lambda a, b, d: (_dot_nt(a, b) * d).astype(BF16), q, k, decay)
    ktd = [(kt_ref[0, c, cols(h), :].astype(F32) * jnp.exp(t - r)).astype(BF16)
           for (c, h), t, r in zip(units, gt_row, g_row)]
    carry = each(lambda t: jnp.exp(t[:, 0:1]), gt_row)
    uwb = each(lambda a: a.astype(BF16), uw)
    kuw = each(_dot, ktd, uwb)
    iuw = each(_dot, intra, uwb)
    lhs = each(lambda a, b, c, e: jnp.concatenate(
        [a[:, DN_HEAD_DIM:], c.astype(F32) * e - b[:, DN_HEAD_DIM:]], axis=0).astype(BF16),
        kuw, iuw, q, eg)

    state = [state_ref[h] for h in range(DN_HEADS)]
    outs = []
    for c in range(blk // cs):
        of = lambda xs: xs[c * DN_HEADS:(c + 1) * DN_HEADS]
        prod = each(lambda a, s: _dot(a, s.astype(BF16)), of(lhs), state)
        outs.append(each(lambda p, b: p[DN_HEAD_DIM:] + b[:, :DN_HEAD_DIM], prod, of(iuw)))
        state = each(lambda s, e, p, a: s * e - p[:DN_HEAD_DIM] + a[:, :DN_HEAD_DIM],
                     state, of(carry), prod, of(kuw))
    for h in range(DN_HEADS):
        state_ref[h] = state[h]
        o = jnp.concatenate([chunk_out[h] for chunk_out in outs], axis=0)
        ms = jnp.mean(o * o, axis=-1, keepdims=True)
        on = o * lax.rsqrt(ms + EPS) * ng_ref[...]
        o_ref[:, cols(h)] = (on * z_ref[:, cols(h)].astype(F32)).astype(BF16)


def _deltanet(dq, dk, dkt, dv, dz, gc, gr, ng, *, batch, seq):
    blk = DN_BLOCK
    n_blk = seq // blk
    tok = pl.BlockSpec((blk, DN_WIDTH), lambda b, c: (b * n_blk + c, 0))
    return pl.pallas_call(
        functools.partial(_dn_kernel, blk=blk),
        out_shape=jax.ShapeDtypeStruct(dq.shape, BF16),
        grid=(batch, n_blk),
        in_specs=[
            tok, tok,
            pl.BlockSpec((1, blk // TR_BLOCK, DN_WIDTH, TR_BLOCK), lambda b, c: (b, c, 0, 0)),
            tok, tok,
            pl.BlockSpec((blk, LANES), lambda b, c: (b * n_blk + c, 0)),
            pl.BlockSpec((1, GATE_ROWS, blk), lambda b, c: (b, 0, c)),
            pl.BlockSpec((1, DN_HEAD_DIM), lambda b, c: (0, 0)),
        ],
        out_specs=tok,
        scratch_shapes=[pltpu.VMEM((DN_HEADS, DN_HEAD_DIM, DN_HEAD_DIM), F32)],
        compiler_params=pltpu.CompilerParams(
            dimension_semantics=("arbitrary", "arbitrary"), vmem_limit_bytes=VMEM_LIMIT),
        name="deltanet",
    )(dq, dk, dkt, dv, dz, gc, gr, ng)


def _outproj_kernel(x_ref, osb_ref, odn_ref, msb_ref, mdn_ref, gate_ref, wsb_ref, wdn_ref, wo_ref, o_ref):
    ysb = _dot(osb_ref[...], wsb_ref[...])
    ydn = _dot(odn_ref[...], wdn_ref[...])
    y = msb_ref[...].astype(F32) * ysb + mdn_ref[...].astype(F32) * ydn
    out = _dot(y.astype(BF16), wo_ref[...])
    o_ref[...] = x_ref[...] + gate_ref[0] * out


def _outproj(x2, osb, odn, msb, mdn, gate, wsb, wdn, wo, *, seq):
    n, d = x2.shape
    tm = OUT_TILE
    assert seq % tm == 0
    s_tiles = seq // tm
    tile = lambda w: pl.BlockSpec((tm, w), lambda t: (t, 0))
    full = lambda a: pl.BlockSpec(a.shape, lambda t: (0,) * a.ndim, pipeline_mode=pl.Buffered(1))
    return pl.pallas_call(
        _outproj_kernel,
        out_shape=jax.ShapeDtypeStruct((n, d), F32),
        grid=(n // tm,),
        in_specs=[tile(d), tile(SB_WIDTH), tile(DN_WIDTH), tile(d), tile(d),
                  pl.BlockSpec((1, 1, d), lambda t: (t // s_tiles, 0, 0)),
                  full(wsb), full(wdn), full(wo)],
        out_specs=tile(d),
        compiler_params=pltpu.CompilerParams(
            dimension_semantics=("arbitrary",), vmem_limit_bytes=VMEM_LIMIT),
        name="outproj",
    )(x2, osb, odn, msb, mdn, gate, wsb, wdn, wo)


def _layer(x2, mod, norm_g, w_in, sb_q_g, sb_k_g, conv_w, dn_a_log, dn_dt_bias, dn_norm_g,
           w_branch_sb, w_branch_dn, w_out, *, batch, seq):
    d = x2.shape[1]
    shift = mod[:batch, 0:d].reshape(batch, 1, d)
    scale = mod[:batch, d:2 * d].reshape(batch, 1, d)
    gate = mod[:batch, 2 * d:3 * d].reshape(batch, 1, d)

    main_cols = 4 * SB_WIDTH + 4 * DN_WIDTH
    wm = w_in[:, :main_cols].astype(BF16)
    w_b = w_in[:, main_cols:main_cols + DN_HEADS]
    w_a = w_in[:, main_cols + DN_HEADS:main_cols + 2 * DN_HEADS]
    pad = jnp.zeros((d, LANES - 3 * DN_HEADS), F32)
    wba = jnp.concatenate([w_b, w_a, w_a, pad], axis=1).astype(BF16)
    wmg = w_in[:, main_cols + 2 * DN_HEADS:].astype(BF16)

    def gate_lanes(v):
        z = jnp.zeros((DN_HEADS,), F32)
        return jnp.concatenate([z, v, v, jnp.zeros((LANES - 3 * DN_HEADS,), F32)]).reshape(1, LANES)

    gq = (jnp.tile(sb_q_g, SB_HEADS) * (LOG2E * SB_HEAD_DIM ** -0.5)).reshape(1, SB_WIDTH)
    gk = jnp.tile(sb_k_g, SB_HEADS).reshape(1, SB_WIDTH)
    head_of = jnp.arange(MXU_TILE) // SB_HEAD_DIM
    grp = jnp.where(head_of[:, None] == head_of[None, :], 1.0 / SB_HEAD_DIM, 0.0).astype(BF16)

    (q, k, vt, z, dq, dk, dkt, dv, dz, gc, gr, msb, mdn) = _inproj(
        x2, scale, shift, norm_g.reshape(1, d), wm, wba, wmg, gq, gk, grp, conv_w,
        gate_lanes(dn_a_log), gate_lanes(dn_dt_bias), batch=batch, seq=seq)
    osb = _attention(q, k, vt, z, batch=batch, seq=seq)
    odn = _deltanet(dq, dk, dkt, dv, dz, gc, gr, dn_norm_g.reshape(1, DN_HEAD_DIM), batch=batch, seq=seq)
    return _outproj(x2, osb, odn, msb, mdn, gate, w_branch_sb.astype(BF16), w_branch_dn.astype(BF16),
                    w_out.astype(BF16), seq=seq)


def kernel(x, c, ada_w, ada_b, norm_g, w_in, sb_q_g, sb_k_g, conv_w, dn_a_log, dn_dt_bias, dn_norm_g,
           w_branch_sb, w_branch_dn, w_out):
    batch, seq, d = x.shape
    depth = ada_w.shape[0]
    assert seq % TOKEN_TILE == 0 and seq % DN_BLOCK == 0 and seq % ATTN_Q == 0
    assert DN_CHUNK == TR_BLOCK and DN_BLOCK % DN_CHUNK == 0 and TOKEN_TILE % DN_CHUNK == 0
    rows = -(-batch // ADALN_ROWS) * ADALN_ROWS
    c_pad = jnp.concatenate([c, jnp.zeros((rows - batch, d), c.dtype)], axis=0) if batch < rows else c
    mod = _adaln(c_pad, ada_w, ada_b)
    x2 = x.reshape(batch * seq, d)
    for l in range(depth):
        x2 = _layer(x2, mod[l], norm_g[l], w_in[l], sb_q_g[l], sb_k_g[l], conv_w[l], dn_a_log[l],
                    dn_dt_bias[l], dn_norm_g[l], w_branch_sb[l], w_branch_dn[l], w_out[l],
                    batch=batch, seq=seq)
    return x2.reshape(batch, seq, d)
```

```python
import functools

import jax
import jax.numpy as jnp
from jax import lax
from jax.experimental import pallas as pl
from jax.experimental.pallas import tpu as pltpu

F32 = jnp.float32
BF16 = jnp.bfloat16

EPS = 1e-6
SB_HEADS = 8
SB_HEAD_DIM = 64
SB_WIDTH = SB_HEADS * SB_HEAD_DIM
DN_HEADS = 4
DN_HEAD_DIM = 128
DN_WIDTH = DN_HEADS * DN_HEAD_DIM
CONV_WIDTH = 4

LANES = 128
MXU_TILE = 256
F32_SUBLANES = 8
HALO_ROWS = 16
GATE_ROWS = 16
TOKEN_TILE = 512
OUT_TILE = 1024
PIPE_DEPTH = 3
ATTN_Q = 256
ATTN_HEADS = 4
ATTN_REPS = 16
TR_BLOCK = 128
DN_BLOCK = 512
DN_CHUNK = 128
INV_BASE = 16
V7X_VMEM_BYTES = 64 * 1024 * 1024
VMEM_LIMIT = V7X_VMEM_BYTES * 7 // 8
ADALN_ROWS = 16
NEG_BIG = -1e30
LOG2E = 1.4426950408889634
EXIT_LOG2 = 152.0


def _dot(a, b):
    return jnp.dot(a, b, preferred_element_type=F32)


def _dot_nt(a, b):
    return lax.dot_general(a, b, (((1,), (1,)), ((), ())), preferred_element_type=F32)


def _sigmoid(x):
    return 0.5 * jnp.tanh(0.5 * x) + 0.5


def _silu(x):
    h = 0.5 * x
    return h + h * jnp.tanh(h)


def _neg_abs(x):
    bits = lax.bitcast_convert_type(x, jnp.uint32) | jnp.uint32(0x80000000)
    return lax.bitcast_convert_type(bits, F32)


def _softplus(x):
    return jnp.maximum(x, 0.0) + jnp.log1p(jnp.exp(-jnp.abs(x)))


def _block_of(idx, size):
    assert size & (size - 1) == 0
    return jnp.right_shift(idx, size.bit_length() - 1)


def _tied(operand, token):
    bits = lax.bitcast_convert_type(token, jnp.uint32)
    zero = jnp.right_shift(jnp.right_shift(bits, 16), 16).astype(F32)
    zero = jnp.concatenate([zero, zero], axis=0).astype(BF16)
    rows, lanes = zero.shape
    first = jnp.concatenate([operand[:rows, :lanes] + zero, operand[:rows, lanes:]], axis=1)
    return jnp.concatenate([first, operand[rows:, :]], axis=0)


def _split3(x):
    hi = x.astype(BF16)
    r = x - hi.astype(F32)
    mid = r.astype(BF16)
    lo = (r - mid.astype(F32)).astype(BF16)
    return hi, mid, lo


def _adaln_kernel(c_ref, w_ref, b_ref, o_ref):
    c = c_ref[...]
    a = c * _sigmoid(c)
    w = w_ref[0]
    a_hi = a.astype(BF16)
    a_lo = (a - a_hi.astype(F32)).astype(BF16)
    w_hi = w.astype(BF16)
    w_lo = (w - w_hi.astype(F32)).astype(BF16)
    o_ref[0] = _dot(a_hi, w_hi) + _dot(a_hi, w_lo) + _dot(a_lo, w_hi) + b_ref[0]


def _adaln(c_pad, ada_w, ada_b):
    depth, d, d3 = ada_w.shape
    rows = c_pad.shape[0]
    tn = 512
    return pl.pallas_call(
        _adaln_kernel,
        out_shape=jax.ShapeDtypeStruct((depth, rows, d3), F32),
        grid=(depth, d3 // tn),
        in_specs=[
            pl.BlockSpec((rows, d), lambda l, j: (0, 0)),
            pl.BlockSpec((1, d, tn), lambda l, j: (l, 0, j)),
            pl.BlockSpec((1, 1, tn), lambda l, j: (l, 0, j)),
        ],
        out_specs=pl.BlockSpec((1, rows, tn), lambda l, j: (l, 0, j)),
        compiler_params=pltpu.CompilerParams(
            dimension_semantics=("arbitrary", "arbitrary"), vmem_limit_bytes=VMEM_LIMIT),
        name="adaln",
    )(c_pad, ada_w, ada_b.reshape(depth, 1, d3))


def _inproj_kernel(x_ref, xh_ref, sc_ref, sh_ref, ng_ref, wm_ref, wba_ref, wmg_ref, gq_ref, gk_ref,
                   grp_ref, cw_ref, alog_ref, dtb_ref,
                   q_ref, k_ref, vt_ref, z_ref, dq_ref, dk_ref, dkt_ref, dv_ref, dz_ref,
                   gc_ref, gr_ref, msb_ref, mdn_ref, *, tm, s_tiles):
    t = pl.program_id(0)
    gmod = ng_ref[...] * (1.0 + sc_ref[0])
    shift = sh_ref[0]

    def norm_mod(xv):
        ms = jnp.mean(xv * xv, axis=-1, keepdims=True)
        return (xv * lax.rsqrt(ms + EPS) * gmod + shift).astype(BF16)

    hb = norm_mod(x_ref[...])
    hh = norm_mod(xh_ref[...])
    seq_start = (t % s_tiles) == 0
    cw = cw_ref[...]
    d_model = msb_ref.shape[1]

    def main(g):
        return wm_ref[:, g * SB_WIDTH:(g + 1) * SB_WIDTH]

    def after(token):
        return hb if token is None else _tied(hb, token)

    def token_of(val):
        return val[:F32_SUBLANES, :LANES].astype(F32)

    def head_rms(p, gain):
        sq = (p * p).astype(BF16)
        ms = jnp.concatenate([_dot(sq[:, a:a + MXU_TILE], grp_ref[...])
                              for a in range(0, SB_WIDTH, MXU_TILE)], axis=1)
        return p * lax.rsqrt(ms + EPS) * gain

    def store_transposed(ref, val):
        for a in range(tm // TR_BLOCK):
            ref[0, a] = val[a * TR_BLOCK:(a + 1) * TR_BLOCK, :].T.astype(BF16)

    row = lax.broadcasted_iota(jnp.int32, (F32_SUBLANES, 1), 0)

    def conv_silu(p_and_halo, slot):
        p, ph = p_and_halo
        taps = cw[:, slot * DN_WIDTH:(slot + 1) * DN_WIDTH]
        acc = p * taps[CONV_WIDTH - 1:CONV_WIDTH, :]
        for back in range(1, CONV_WIDTH):
            shifted = pltpu.roll(p, back, axis=0)
            first = jnp.where(row < back, pltpu.roll(ph, back, axis=0), shifted[:F32_SUBLANES, :])
            src = jnp.concatenate([first, shifted[F32_SUBLANES:, :]], axis=0)
            acc = acc + src * taps[CONV_WIDTH - 1 - back:CONV_WIDTH - back, :]
        return _silu(acc)

    def l2n(y, scale):
        outs = []
        for h in range(DN_HEADS):
            yh = y[:, h * DN_HEAD_DIM:(h + 1) * DN_HEAD_DIM]
            ss = jnp.sum(yh * yh, axis=-1, keepdims=True)
            outs.append(yh * (lax.rsqrt(ss + EPS) * scale))
        return jnp.concatenate(outs, axis=1)

    def gates(pba):
        beta = _sigmoid(pba)
        ld = -jnp.exp(alog_ref[...]) * _softplus(pba + dtb_ref[...])
        lane = lax.broadcasted_iota(jnp.int32, (tm, LANES), 1)
        ld = jnp.where((lane >= DN_HEADS) & (lane < 3 * DN_HEADS), ld, 0.0)
        ri = lax.broadcasted_iota(jnp.int32, (tm, tm), 0)
        ci = lax.broadcasted_iota(jnp.int32, (tm, tm), 1)
        same = _block_of(ri, DN_CHUNK) == _block_of(ci, DN_CHUNK)
        lower = jnp.where(same & (ci <= ri), 1.0, 0.0).astype(BF16)
        wide = _dot(lower, jnp.concatenate(_split3(ld), axis=1))
        gcum = wide[:, 0:LANES] + wide[:, LANES:2 * LANES] + wide[:, 2 * LANES:]
        gtot = jnp.concatenate(
            [jnp.broadcast_to(gcum[c * DN_CHUNK + DN_CHUNK - 1:(c + 1) * DN_CHUNK, :], (DN_CHUNK, LANES))
             for c in range(tm // DN_CHUNK)], axis=0)
        g = jnp.where(lane < DN_HEADS, beta, jnp.where(lane < 2 * DN_HEADS, gcum, gtot))
        gc_ref[...] = g
        gr_ref[0] = g.T[0:GATE_ROWS, :]
        return g

    def put(ref, fn):
        def epilogue(p):
            val = fn(p)
            ref[...] = val.astype(ref.dtype)
            return token_of(val)
        return epilogue

    def project(weights):
        return lambda token: _dot(after(token), weights())

    def project_with_halo(g):
        def run(token):
            res = _dot(jnp.concatenate([hh, after(token)], axis=0), main(g))
            halo = res[HALO_ROWS - F32_SUBLANES:HALO_ROWS, :]
            return res[HALO_ROWS:, :], jnp.where(seq_start, 0.0, halo)
        return run

    def value_t(p):
        store_transposed(vt_ref, p)
        return token_of(p)

    def delta_k(p_and_halo):
        yk = l2n(conv_silu(p_and_halo, 1), 1.0)
        dk_ref[...] = yk.astype(BF16)
        store_transposed(dkt_ref, yk)
        return token_of(yk)

    stages = [
        (project_with_halo(4), put(dq_ref, lambda ph: l2n(conv_silu(ph, 0), DN_HEAD_DIM ** -0.5))),
        (project(lambda: wba_ref[...]), lambda p: token_of(gates(p))),
        (project(lambda: main(0)), put(q_ref, lambda p: head_rms(p, gq_ref[...]))),
        (project_with_halo(5), delta_k),
        (project(lambda: main(1)), put(k_ref, lambda p: head_rms(p, gk_ref[...]))),
        (project_with_halo(6), put(dv_ref, lambda ph: conv_silu(ph, 2))),
        (project(lambda: main(2)), value_t),
        (project(lambda: wmg_ref[:, 0:d_model]), put(msb_ref, _sigmoid)),
        (project(lambda: main(3)), put(z_ref, _silu)),
        (project(lambda: wmg_ref[:, d_model:2 * d_model]), put(mdn_ref, _sigmoid)),
        (project(lambda: main(7)), put(dz_ref, _silu)),
    ]
    tokens, pending = [], [stages[g][0](None) for g in range(PIPE_DEPTH)]
    for index in range(len(stages)):
        tokens.append(stages[index][1](pending[index]))
        ahead = index + PIPE_DEPTH
        if ahead < len(stages):
            pending.append(stages[ahead][0](tokens[index]))


def _inproj(x2, scale, shift, norm_g, wm, wba, wmg, gq, gk, grp, cw, alog, dtb, *, batch, seq):
    n, d = x2.shape
    tm = TOKEN_TILE
    s_tiles = seq // tm
    n_sub = tm // TR_BLOCK
    n_blk = seq // TR_BLOCK
    tile = lambda w: pl.BlockSpec((tm, w), lambda t: (t, 0))
    full = lambda a: pl.BlockSpec(a.shape, lambda t: (0,) * a.ndim, pipeline_mode=pl.Buffered(1))
    per_batch = pl.BlockSpec((1, 1, d), lambda t: (t // s_tiles, 0, 0))
    tr_spec = pl.BlockSpec((1, n_sub, SB_WIDTH, TR_BLOCK), lambda t: (t // s_tiles, t % s_tiles, 0, 0))
    tr_shape = jax.ShapeDtypeStruct((batch, n_blk, SB_WIDTH, TR_BLOCK), BF16)
    tok = lambda w, dt=BF16: jax.ShapeDtypeStruct((n, w), dt)
    out_shape = (
        tok(SB_WIDTH), tok(SB_WIDTH), tr_shape, tok(SB_WIDTH),
        tok(DN_WIDTH), tok(DN_WIDTH), tr_shape, tok(DN_WIDTH), tok(DN_WIDTH),
        tok(LANES, F32),
        jax.ShapeDtypeStruct((batch, GATE_ROWS, seq), F32),
        tok(d), tok(d),
    )
    out_specs = (
        tile(SB_WIDTH), tile(SB_WIDTH), tr_spec, tile(SB_WIDTH),
        tile(DN_WIDTH), tile(DN_WIDTH), tr_spec, tile(DN_WIDTH), tile(DN_WIDTH),
        tile(LANES),
        pl.BlockSpec((1, GATE_ROWS, tm), lambda t: (t // s_tiles, 0, t % s_tiles)),
        tile(d), tile(d),
    )
    halo_blocks = tm // HALO_ROWS
    in_specs = [
        tile(d),
        pl.BlockSpec((HALO_ROWS, d), lambda t: (jnp.maximum(t * halo_blocks - 1, 0), 0)),
        per_batch, per_batch, full(norm_g), full(wm), full(wba), full(wmg), full(gq), full(gk),
        full(grp), full(cw), full(alog), full(dtb),
    ]
    return pl.pallas_call(
        functools.partial(_inproj_kernel, tm=tm, s_tiles=s_tiles),
        out_shape=out_shape,
        grid=(n // tm,),
        in_specs=in_specs,
        out_specs=out_specs,
        compiler_params=pltpu.CompilerParams(
            dimension_semantics=("arbitrary",), vmem_limit_bytes=VMEM_LIMIT),
        name="inproj",
    )(x2, x2, scale, shift, norm_g, wm, wba, wmg, gq, gk, grp, cw, alog, dtb)


def _attn_kernel(q_ref, k_ref, vt_ref, z_ref, o_ref, *, tq, tk, reps):
    def one(r, carry):
        rows = pl.ds(pl.multiple_of(r * tq, tq), tq)
        _attn_block(pl.program_id(2) * reps + r, q_ref.at[rows], k_ref, vt_ref, z_ref.at[rows],
                    o_ref.at[rows], tq=tq, tk=tk)
        return carry

    lax.fori_loop(0, reps, one, 0)


def _attn_block(i, q_ref, k_ref, vt_ref, z_ref, o_ref, *, tq, tk):
    width = q_ref.shape[1]
    heads = width // SB_HEAD_DIM
    ks = lax.broadcasted_iota(jnp.int32, (tk, tq), 0)
    qt = lax.broadcasted_iota(jnp.int32, (tk, tq), 1)
    causal = ks < qt
    ur = lax.broadcasted_iota(jnp.int32, (tk, tk), 0)
    uc = lax.broadcasted_iota(jnp.int32, (tk, tk), 1)
    upper = jnp.where(uc > ur, 1.0, 0.0).astype(BF16)
    lane = lax.broadcasted_iota(jnp.int32, (tq, width), 1)
    q = q_ref[...]
    qhs = [jnp.where(_block_of(lane, SB_HEAD_DIM) == hh, q, jnp.zeros_like(q)) for hh in range(heads)]

    def scores(units):
        st = [_dot_nt(k_ref[0, pl.ds(pl.multiple_of(j * tk, tk), tk), :], qh) for j, qh, _ in units]
        sp = [jnp.maximum(s, 0.0) + jnp.log(1.0 + jnp.exp2(_neg_abs(s))) * LOG2E for s in st]
        spm = [p if u[2] is None else jnp.where(u[2], p, 0.0) for p, u in zip(sp, units)]
        later = [_dot(upper, m.astype(BF16)) for m in spm]
        base = [s - p - l for s, p, l in zip(st, sp, later)]
        sums = [l[0:1, :] + m[0:1, :] for l, m in zip(later, spm)]
        return base, sums

    def weights(base, carry, mask):
        w = jnp.exp2(base - carry)
        if mask is not None:
            w = jnp.where(mask, w, 0.0)
        return w.astype(BF16)

    def values(j, hh):
        return vt_ref[0, j, hh * SB_HEAD_DIM:(hh + 1) * SB_HEAD_DIM, :]

    j_hi = 2 * i + 1
    j_lo = 2 * i
    causal_hi = causal[:, :tk]

    def diagonal_units():
        units = []
        for hh in range(heads):
            units += [(j_hi, qhs[hh][tk:, :], causal_hi), (j_lo, qhs[hh], causal)]
        return units

    def diagonal_weights(base, sums):
        w_hi, w_lo, carry = [], [], []
        for hh in range(heads):
            carry_hi = jnp.concatenate([jnp.zeros((1, tk), F32), sums[2 * hh]], axis=1)
            w_hi.append(weights(base[2 * hh], 0.0, causal_hi))
            w_lo.append(weights(base[2 * hh + 1], carry_hi, causal))
            carry.append(carry_hi + sums[2 * hh + 1])
        return w_hi, w_lo, carry

    def pair_units(j):
        units = []
        for hh in range(heads):
            units += [(j, qhs[hh], None), (j - 1, qhs[hh], None)]
        return units

    def pair_weights(base, sums, carry_in):
        ws, carry_out = [], []
        for hh in range(heads):
            w_near = weights(base[2 * hh], carry_in[hh], None)
            carry = carry_in[hh] + sums[2 * hh]
            w_far = weights(base[2 * hh + 1], carry, None)
            ws.append(jnp.concatenate([w_near, w_far], axis=0))
            carry_out.append(carry + sums[2 * hh + 1])
        return ws, carry_out

    def pair_values(j, hh):
        return jnp.concatenate([values(j, hh), values(j - 1, hh)], axis=1)

    def widen(acc_hi):
        return jnp.concatenate([jnp.zeros((SB_HEAD_DIM, tk), F32), acc_hi], axis=1)

    def finish(acc):
        o = jnp.concatenate(acc, axis=0).astype(BF16).T
        o_ref[...] = (o.astype(F32) * z_ref[...].astype(F32)).astype(BF16)

    @pl.when(i == 0)
    def _():
        base, sums = scores(diagonal_units())
        w_hi, w_lo, _ = diagonal_weights(base, sums)
        acc_hi = [_dot(values(j_hi, hh), w_hi[hh]) for hh in range(heads)]
        acc_lo = [_dot(values(j_lo, hh), w_lo[hh]) for hh in range(heads)]
        finish([lo + widen(hi) for lo, hi in zip(acc_lo, acc_hi)])

    @pl.when(i > 0)
    def _():
        n_diag = 2 * heads
        j_c, j_d = 2 * i - 1, 2 * i - 2
        near = [(j_c, qhs[hh], None) for hh in range(heads)]
        far_lower = [(j_d, qhs[hh][:tk, :], None) for hh in range(heads)]
        base, sums = scores(diagonal_units() + near + far_lower)
        w_hi, w_lo, carry = diagonal_weights(base[:n_diag], sums[:n_diag])
        base_c, sums_c = base[n_diag:n_diag + heads], sums[n_diag:n_diag + heads]
        base_d, sums_d = base[n_diag + heads:], sums[n_diag + heads:]
        w_c = [weights(b, c, None) for b, c in zip(base_c, carry)]
        carry = [c + s for c, s in zip(carry, sums_c)]
        w_d = [weights(b, c[:, :tk], None) for b, c in zip(base_d, carry)]
        carry_lower = [c[:, :tk] + s for c, s in zip(carry, sums_d)]
        carry_upper = [c[:, tk:] for c in carry]
        acc_hi = [_dot(values(j_hi, hh), w_hi[hh]) for hh in range(heads)]
        acc_lower = [_dot(values(j_d, hh), w_d[hh]) for hh in range(heads)]
        acc_rest = [_dot(jnp.concatenate([values(j_lo, hh), values(j_c, hh)], axis=1),
                         jnp.concatenate([w_lo[hh], w_c[hh]], axis=0)) for hh in range(heads)]
        acc = [rest + jnp.concatenate([lower, hi], axis=1)
               for rest, lower, hi in zip(acc_rest, acc_lower, acc_hi)]

        def least(carry):
            return jnp.min(functools.reduce(jnp.minimum, carry))

        def upper_far(carry_upper, acc, _):
            base, sums = scores([(j_d, qhs[hh][tk:, :], None) for hh in range(heads)])
            ws = [weights(b, c, None) for b, c in zip(base, carry_upper)]
            extra = [_dot(values(j_d, hh), ws[hh]) for hh in range(heads)]
            carry_upper = [c + s for c, s in zip(carry_upper, sums)]
            return carry_upper, [a + widen(e) for a, e in zip(acc, extra)], least(carry_upper)

        least_lower, least_upper = least(carry_lower), least(carry_upper)
        carry_upper, acc, least_upper = lax.cond(least_upper < EXIT_LOG2, upper_far,
                                                 lambda c, a, m: (c, a, m), carry_upper, acc, least_upper)
        carry = [jnp.concatenate([lo, up], axis=1) for lo, up in zip(carry_lower, carry_upper)]

        def cond(st):
            return (st[0] >= 1) & (st[1] < EXIT_LOG2)

        def body(st):
            j, carry, acc = st[0], list(st[2:2 + heads]), list(st[2 + heads:])
            base, sums = scores(pair_units(j))
            ws, carry = pair_weights(base, sums, carry)
            acc = [a + _dot(pair_values(j, hh), ws[hh]) for hh, a in enumerate(acc)]
            return (j - 2, least(carry), *carry, *acc)

        final = lax.while_loop(cond, body, (2 * i - 3, jnp.minimum(least_lower, least_upper), *carry, *acc))
        finish(list(final[2 + heads:]))


def _attention(q, k, vt, z, *, batch, seq):
    tq, tk = ATTN_Q, TR_BLOCK
    width = ATTN_HEADS * SB_HEAD_DIM
    assert tq == 2 * tk and width % LANES == 0 and SB_WIDTH % width == 0
    reps = min(ATTN_REPS, seq // tq)
    assert seq % (tq * reps) == 0
    n_steps = seq // (tq * reps)
    groups = SB_WIDTH // width
    k3 = k.reshape(batch, seq, SB_WIDTH)
    qspec = pl.BlockSpec((tq * reps, width), lambda b, p, i: (b * n_steps + i, p))
    return pl.pallas_call(
        functools.partial(_attn_kernel, tq=tq, tk=tk, reps=reps),
        out_shape=jax.ShapeDtypeStruct(q.shape, BF16),
        grid=(batch, groups, n_steps),
        in_specs=[
            qspec,
            pl.BlockSpec((1, seq, width), lambda b, p, i: (b, 0, p)),
            pl.BlockSpec((1, seq // tk, width, tk), lambda b, p, i: (b, 0, p, 0)),
            qspec,
        ],
        out_specs=qspec,
        compiler_params=pltpu.CompilerParams(
            dimension_semantics=("arbitrary", "arbitrary", "arbitrary"), vmem_limit_bytes=VMEM_LIMIT),
        name="sb_attention",
    )(q, k3, vt, z)


def _inverse_minus_identity(ms, ri, ci, size):
    mm = lambda a, b: _dot(a.astype(BF16), b.astype(BF16))
    each = lambda f, *lists: [f(*xs) for xs in zip(*lists)]
    base = _block_of(ri, INV_BASE) == _block_of(ci, INV_BASE)
    md = each(lambda m: jnp.where(base, m, 0.0), ms)
    m2 = each(lambda a: mm(a, a), md)
    m4 = each(lambda a: mm(a, a), m2)
    m8 = each(lambda a: mm(a, a), m4)
    n = each(lambda a, b: b - a - mm(a, b), md, m2)
    n = each(lambda a, b: a + b + mm(a, b), n, m4)
    n = each(lambda a, b: a + b + mm(a, b), n, m8)
    width = INV_BASE
    while width < size:
        inner = _block_of(ri, width) == _block_of(ci, width)
        outer = _block_of(ri, 2 * width) == _block_of(ci, 2 * width)
        below = outer & jnp.logical_not(inner)
        mo = each(lambda m: jnp.where(below, m, 0.0), ms)
        x = each(lambda a, b: b + mm(a, b), n, mo)
        n = each(lambda a, b: a - b - mm(b, a), n, x)
        width *= 2
    return n


def _dn_kernel(q_ref, k_ref, kt_ref, v_ref, z_ref, gc_ref, gr_ref, ng_ref, o_ref, state_ref, *, blk):
    @pl.when(pl.program_id(1) == 0)
    def _():
        state_ref[...] = jnp.zeros_like(state_ref)

    cs = DN_CHUNK
    ri = lax.broadcasted_iota(jnp.int32, (cs, cs), 0)
    ci = lax.broadcasted_iota(jnp.int32, (cs, cs), 1)
    tril = ci <= ri
    strict = ci < ri
    gates = gc_ref[...]
    grows = gr_ref[0]
    each = lambda f, *lists: [f(*xs) for xs in zip(*lists)]
    units = [(c, h) for c in range(blk // cs) for h in range(DN_HEADS)]
    rows = lambda c: slice(c * cs, (c + 1) * cs)
    cols = lambda h: slice(h * DN_HEAD_DIM, (h + 1) * DN_HEAD_DIM)
    q = [q_ref[rows(c), cols(h)] for c, h in units]
    k = [k_ref[rows(c), cols(h)] for c, h in units]
    beta = [gates[rows(c), h:h + 1] for c, h in units]
    g = [gates[rows(c), DN_HEADS + h:DN_HEADS + h + 1] for c, h in units]
    g_row = [grows[DN_HEADS + h:DN_HEADS + h + 1, rows(c)] for c, h in units]
    gt_row = [grows[2 * DN_HEADS + h:2 * DN_HEADS + h + 1, rows(c)] for c, h in units]
    eg = each(jnp.exp, g)
    kb = each(lambda a, b: a.astype(F32) * b, k, beta)
    vb = [v_ref[rows(c), cols(h)].astype(F32) * b for (c, h), b in zip(units, beta)]
    decay = each(lambda a, b: jnp.exp(jnp.where(tril, a - b, NEG_BIG)), g, g_row)
    m = each(lambda a, b, d: jnp.where(strict, _dot_nt(a.astype(BF16), b) * d, 0.0), kb, k, decay)
    n = each(lambda a: a.astype(BF16), _inverse_minus_identity(m, ri, ci, cs))
    uw = each(lambda a, b, e: jnp.concatenate([a, b * e], axis=1), vb, kb, eg)
    uw = each(lambda a, b: b + _dot(a, b.astype(BF16)), n, uw)
    intra = each(lambda a, b, d: (_dot_nt(a, b) * d).astype(BF16), q, k, decay)
    ktd = [(kt_ref[0, c, cols(h), :].astype(F32) * jnp.exp(t - r)).astype(BF16)
           for (c, h), t, r in zip(units, gt_row, g_row)]
    carry = each(lambda t: jnp.exp(t[:, 0:1]), gt_row)
    uwb = each(lambda a: a.astype(BF16), uw)
    kuw = each(_dot, ktd, uwb)
    iuw = each(_dot, intra, uwb)
    lhs = each(lambda a, b, c, e: jnp.concatenate(
        [a[:, DN_HEAD_DIM:], c.astype(F32) * e - b[:, DN_HEAD_DIM:]], axis=0).astype(BF16),
        kuw, iuw, q, eg)

    dk = DN_HEAD_DIM
    n_pairs = blk // (2 * cs)
    pair = lambda xs, p, k: xs[(2 * p + k) * DN_HEADS:(2 * p + k + 1) * DN_HEADS]
    cross = [each(lambda a, b: _dot(a[:, dk:].astype(BF16), b.astype(BF16)), pair(kuw, p, 1), pair(kuw, p, 0))
             for p in range(n_pairs)]
    state = [state_ref[h] for h in range(DN_HEADS)]
    outs = []
    for p in range(n_pairs):
        ku0, ku1, e0, e1 = pair(kuw, p, 0), pair(kuw, p, 1), pair(carry, p, 0), pair(carry, p, 1)
        g2 = each(lambda a, b, x, ea, eb: eb * a[:, dk:] + ea * b[:, dk:] - x[:, dk:], ku0, ku1, cross[p], e0, e1)
        r2 = each(lambda a, b, x, eb: eb * a[:, :dk] - x[:, :dk] + b[:, :dk], ku0, ku1, cross[p], e1)
        stacked = each(lambda a, g: jnp.concatenate([a, g.astype(BF16)], axis=0), pair(lhs, p, 0), g2)
        prod = each(lambda a, s: _dot(a, s.astype(BF16)), stacked, state)
        mid = each(lambda s, e, pr, a: s * e - pr[:dk] + a[:, :dk], state, e0, prod, ku0)
        outs.append(each(lambda pr, b: pr[dk:dk + cs] + b[:, :dk], prod, pair(iuw, p, 0)))
        outs.append(each(lambda a, s, b: _dot(a[dk:], s.astype(BF16)) + b[:, :dk],
                         pair(lhs, p, 1), mid, pair(iuw, p, 1)))
        state = each(lambda s, ea, eb, pr, r: s * (ea * eb) - pr[dk + cs:] + r, state, e0, e1, prod, r2)
    for h in range(DN_HEADS):
        state_ref[h] = state[h]
        o = jnp.concatenate([chunk_out[h] for chunk_out in outs], axis=0)
        ms = jnp.mean(o * o, axis=-1, keepdims=True)
        on = o * lax.rsqrt(ms + EPS) * ng_ref[...]
        o_ref[:, cols(h)] = (on * z_ref[:, cols(h)].astype(F32)).astype(BF16)


def _deltanet(dq, dk, dkt, dv, dz, gc, gr, ng, *, batch, seq):
    blk = DN_BLOCK
    n_blk = seq // blk
    tok = pl.BlockSpec((blk, DN_WIDTH), lambda b, c: (b * n_blk + c, 0))
    return pl.pallas_call(
        functools.partial(_dn_kernel, blk=blk),
        out_shape=jax.ShapeDtypeStruct(dq.shape, BF16),
        grid=(batch, n_blk),
        in_specs=[
            tok, tok,
            pl.BlockSpec((1, blk // TR_BLOCK, DN_WIDTH, TR_BLOCK), lambda b, c: (b, c, 0, 0)),
            tok, tok,
            pl.BlockSpec((blk, LANES), lambda b, c: (b * n_blk + c, 0)),
            pl.BlockSpec((1, GATE_ROWS, blk), lambda b, c: (b, 0, c)),
            pl.BlockSpec((1, DN_HEAD_DIM), lambda b, c: (0, 0)),
        ],
        out_specs=tok,
        scratch_shapes=[pltpu.VMEM((DN_HEADS, DN_HEAD_DIM, DN_HEAD_DIM), F32)],
        compiler_params=pltpu.CompilerParams(
            dimension_semantics=("arbitrary", "arbitrary"), vmem_limit_bytes=VMEM_LIMIT),
        name="deltanet",
    )(dq, dk, dkt, dv, dz, gc, gr, ng)


def _outproj_kernel(x_ref, osb_ref, odn_ref, msb_ref, mdn_ref, gate_ref, wsb_ref, wdn_ref, wo_ref, o_ref):
    ysb = _dot(osb_ref[...], wsb_ref[...])
    ydn = _dot(odn_ref[...], wdn_ref[...])
    y = msb_ref[...].astype(F32) * ysb + mdn_ref[...].astype(F32) * ydn
    out = _dot(y.astype(BF16), wo_ref[...])
    o_ref[...] = x_ref[...] + gate_ref[0] * out


def _outproj(x2, osb, odn, msb, mdn, gate, wsb, wdn, wo, *, seq):
    n, d = x2.shape
    tm = OUT_TILE
    assert seq % tm == 0
    s_tiles = seq // tm
    tile = lambda w: pl.BlockSpec((tm, w), lambda t: (t, 0))
    full = lambda a: pl.BlockSpec(a.shape, lambda t: (0,) * a.ndim, pipeline_mode=pl.Buffered(1))
    return pl.pallas_call(
        _outproj_kernel,
        out_shape=jax.ShapeDtypeStruct((n, d), F32),
        grid=(n // tm,),
        in_specs=[tile(d), tile(SB_WIDTH), tile(DN_WIDTH), tile(d), tile(d),
                  pl.BlockSpec((1, 1, d), lambda t: (t // s_tiles, 0, 0)),
                  full(wsb), full(wdn), full(wo)],
        out_specs=tile(d),
        compiler_params=pltpu.CompilerParams(
            dimension_semantics=("arbitrary",), vmem_limit_bytes=VMEM_LIMIT),
        name="outproj",
    )(x2, osb, odn, msb, mdn, gate, wsb, wdn, wo)


def _layer(x2, mod, norm_g, w_in, sb_q_g, sb_k_g, conv_w, dn_a_log, dn_dt_bias, dn_norm_g,
           w_branch_sb, w_branch_dn, w_out, *, batch, seq):
    d = x2.shape[1]
    shift = mod[:batch, 0:d].reshape(batch, 1, d)
    scale = mod[:batch, d:2 * d].reshape(batch, 1, d)
    gate = mod[:batch, 2 * d:3 * d].reshape(batch, 1, d)

    main_cols = 4 * SB_WIDTH + 4 * DN_WIDTH
    wm = w_in[:, :main_cols].astype(BF16)
    w_b = w_in[:, main_cols:main_cols + DN_HEADS]
    w_a = w_in[:, main_cols + DN_HEADS:main_cols + 2 * DN_HEADS]
    pad = jnp.zeros((d, LANES - 3 * DN_HEADS), F32)
    wba = jnp.concatenate([w_b, w_a, w_a, pad], axis=1).astype(BF16)
    wmg = w_in[:, main_cols + 2 * DN_HEADS:].astype(BF16)

    def gate_lanes(v):
        z = jnp.zeros((DN_HEADS,), F32)
        return jnp.concatenate([z, v, v, jnp.zeros((LANES - 3 * DN_HEADS,), F32)]).reshape(1, LANES)

    gq = (jnp.tile(sb_q_g, SB_HEADS) * (LOG2E * SB_HEAD_DIM ** -0.5)).reshape(1, SB_WIDTH)
    gk = jnp.tile(sb_k_g, SB_HEADS).reshape(1, SB_WIDTH)
    head_of = jnp.arange(MXU_TILE) // SB_HEAD_DIM
    grp = jnp.where(head_of[:, None] == head_of[None, :], 1.0 / SB_HEAD_DIM, 0.0).astype(BF16)

    (q, k, vt, z, dq, dk, dkt, dv, dz, gc, gr, msb, mdn) = _inproj(
        x2, scale, shift, norm_g.reshape(1, d), wm, wba, wmg, gq, gk, grp, conv_w,
        gate_lanes(dn_a_log), gate_lanes(dn_dt_bias), batch=batch, seq=seq)
    osb = _attention(q, k, vt, z, batch=batch, seq=seq)
    odn = _deltanet(dq, dk, dkt, dv, dz, gc, gr, dn_norm_g.reshape(1, DN_HEAD_DIM), batch=batch, seq=seq)
    return _outproj(x2, osb, odn, msb, mdn, gate, w_branch_sb.astype(BF16), w_branch_dn.astype(BF16),
                    w_out.astype(BF16), seq=seq)


def kernel(x, c, ada_w, ada_b, norm_g, w_in, sb_q_g, sb_k_g, conv_w, dn_a_log, dn_dt_bias, dn_norm_g,
           w_branch_sb, w_branch_dn, w_out):
    batch, seq, d = x.shape
    depth = ada_w.shape[0]
    assert seq % TOKEN_TILE == 0 and seq % DN_BLOCK == 0 and seq % ATTN_Q == 0
    assert DN_CHUNK == TR_BLOCK and DN_BLOCK % DN_CHUNK == 0 and TOKEN_TILE % DN_CHUNK == 0
    rows = -(-batch // ADALN_ROWS) * ADALN_ROWS
    c_pad = jnp.concatenate([c, jnp.zeros((rows - batch, d), c.dtype)], axis=0) if batch < rows else c
    mod = _adaln(c_pad, ada_w, ada_b)
    x2 = x.reshape(batch * seq, d)
    for l in range(depth):
        x2 = _layer(x2, mod[l], norm_g[l], w_in[l], sb_q_g[l], sb_k_g[l], conv_w[l], dn_a_log[l],
                    dn_dt_bias[l], dn_norm_g[l], w_branch_sb[l], w_branch_dn[l], w_out[l],
                    batch=batch, seq=seq)
    return x2.reshape(batch, seq, d)
```
